```python
import math
import jax, jax.numpy as jnp
from jax import lax
import numpy as np

D_MODEL = 1024
BATCH = 8
SEQ = 4096
DEPTH = 4

CHUNK = 64
N_MEM = 256
EPS = 1e-6
NEG_BIG = -1e30
F_FLOOR = 1e-20
N_MIXERS = 2
HG_EXPAND = 128
HG_HEADS = D_MODEL // HG_EXPAND
HG_DK = HG_EXPAND
HG_DV = D_MODEL // HG_HEADS
HG_IN = 4 * D_MODEL
ML_HEADS = 8
ML_DV = D_MODEL // ML_HEADS
ML_DQK = ML_DV // 2
ML_IN = 2 * ML_HEADS * ML_DQK + 2 * D_MODEL + 2 * ML_HEADS
XA_HEADS = 4
XA_HD = D_MODEL // XA_HEADS
D_FF = 2816
CONV_W = 3
N_A = (DEPTH + 1) // 2
N_B = DEPTH // 2

kernel_name = "hybrid_hgrn2_mlstm_memxattn_convffn"


def rmsnorm(x, g):
    xf = x.astype(jnp.float32)
    y = xf * lax.rsqrt(jnp.mean(xf * xf, axis=-1, keepdims=True) + EPS)
    return (y * g.astype(jnp.float32)).astype(x.dtype)


def head_rmsnorm(o, g):
    B, S, H, d = o.shape
    o = o * lax.rsqrt(jnp.mean(o * o, axis=-1, keepdims=True) + EPS)
    return o.reshape(B, S, H * d) * g.astype(jnp.float32)


def to_chunks(t, n_heads):
    B, S, _ = t.shape
    return t.reshape(B, S // CHUNK, CHUNK, n_heads, -1).transpose(1, 0, 3, 2, 4)


def from_chunks(t):
    nC, B, H, L, d = t.shape
    return t.transpose(1, 0, 3, 2, 4).reshape(B, nC * L, H, d)


def hgrn2_mixer(a, w_in, w_out, norm_g, lb):
    B, S, _ = a.shape
    q, z, v, g = jnp.split(a @ w_in, 4, axis=-1)
    q = jax.nn.silu(q.astype(jnp.float32))
    zf = z.astype(jnp.float32)
    lbf = lb.astype(jnp.float32)
    f = lbf + (1.0 - lbf) * jax.nn.sigmoid(zf)
    log_f = jnp.log(jnp.maximum(f, F_FLOOR))
    k = (1.0 - lbf) * jax.nn.sigmoid(-zf)
    qc = to_chunks(q, HG_HEADS)
    kc = to_chunks(k, HG_HEADS)
    vc = to_chunks(v.astype(jnp.float32), HG_HEADS)
    fc = to_chunks(log_f, HG_HEADS)
    causal = jnp.tril(jnp.ones((CHUNK, CHUNK), dtype=bool))

    def step(S_prev, inp):
        qb, kb, vb, lfb = inp
        b = jnp.cumsum(lfb, axis=2)
        diff = b[:, :, :, None, :] - b[:, :, None, :, :]
        decay = jnp.exp(jnp.where(causal[None, None, :, :, None], diff, NEG_BIG))
        A = jnp.einsum('bhtk,bhtsk,bhsk->bhts', qb, decay, kb)
        o = (jnp.einsum('bhts,bhsv->bhtv', A, vb)
             + jnp.einsum('bhtk,bhkv->bhtv', qb * jnp.exp(b), S_prev))
        bL = b[:, :, -1:, :]
        S_new = (jnp.exp(bL[:, :, 0, :])[..., None] * S_prev
                 + jnp.einsum('bhsk,bhsv->bhkv', kb * jnp.exp(bL - b), vb))
        return S_new, o

    S0 = jnp.zeros((B, HG_HEADS, HG_DK, HG_DV), jnp.float32)
    _, oc = lax.scan(step, S0, (qc, kc, vc, fc))
    o = head_rmsnorm(from_chunks(oc), norm_g) * jax.nn.silu(g.astype(jnp.float32))
    return o.astype(a.dtype) @ w_out


def mlstm_mixer(a, w_in, b_gate, w_out, norm_g):
    B, S, _ = a.shape
    nqk = ML_HEADS * ML_DQK
    proj = a @ w_in
    q, k, v, o_pre, gates = jnp.split(
        proj, [nqk, 2 * nqk, 2 * nqk + D_MODEL, 2 * nqk + 2 * D_MODEL], axis=-1)
    gates = gates.astype(jnp.float32) + b_gate.astype(jnp.float32)
    log_i = gates[..., :ML_HEADS]
    log_f = jax.nn.log_sigmoid(gates[..., ML_HEADS:])
    qc = to_chunks(q.astype(jnp.float32), ML_HEADS)
    kc = to_chunks(k.astype(jnp.float32) * (ML_DQK ** -0.5), ML_HEADS)
    vc = to_chunks(v.astype(jnp.float32), ML_HEADS)
    ic = to_chunks(log_i, ML_HEADS)[..., 0]
    fc = to_chunks(log_f, ML_HEADS)[..., 0]
    causal = jnp.tril(jnp.ones((CHUNK, CHUNK), dtype=bool))

    def step(carry, inp):
        C, n, m = carry
        qb, kb, vb, lf, li = inp
        b = jnp.cumsum(lf, axis=-1)
        logD = jnp.where(causal, b[..., :, None] - b[..., None, :] + li[..., None, :],
                         NEG_BIG)
        inter = b + m[..., None]
        m_t = jnp.maximum(inter, jnp.max(logD, axis=-1))
        Dm = jnp.exp(logD - m_t[..., None])
        w_inter = jnp.exp(inter - m_t)
        s = jnp.einsum('bhtd,bhsd->bhts', qb, kb) * Dm
        num = (jnp.einsum('bhts,bhsv->bhtv', s, vb)
               + w_inter[..., None] * jnp.einsum('bhtd,bhdv->bhtv', qb, C))
        den = jnp.sum(s, axis=-1) + w_inter * jnp.einsum('bhtd,bhd->bht', qb, n)
        h = num / jnp.maximum(jnp.abs(den), jnp.exp(-m_t))[..., None]
        bL = b[..., -1]
        logw = bL[..., None] - b + li
        m_new = jnp.maximum(bL + m, jnp.max(logw, axis=-1))
        w = jnp.exp(logw - m_new[..., None])
        dec = jnp.exp(bL + m - m_new)
        C_new = dec[..., None, None] * C + jnp.einsum('bhs,bhsd,bhsv->bhdv', w, kb, vb)
        n_new = dec[..., None] * n + jnp.einsum('bhs,bhsd->bhd', w, kb)
        return (C_new, n_new, m_new), h

    carry0 = (jnp.zeros((B, ML_HEADS, ML_DQK, ML_DV), jnp.float32),
              jnp.zeros((B, ML_HEADS, ML_DQK), jnp.float32),
              jnp.zeros((B, ML_HEADS), jnp.float32))
    _, hc = lax.scan(step, carry0, (qc, kc, vc, fc, ic))
    o = head_rmsnorm(from_chunks(hc), norm_g) * jax.nn.sigmoid(o_pre.astype(jnp.float32))
    return o.astype(a.dtype) @ w_out


def memory_cross_attn(a, memn, wq, wkv, wo):
    B, S, _ = a.shape
    q = (a @ wq).reshape(B, S, XA_HEADS, XA_HD)
    k, v = jnp.split(memn @ wkv, 2, axis=-1)
    k = k.reshape(B, -1, XA_HEADS, XA_HD)
    v = v.reshape(B, -1, XA_HEADS, XA_HD)
    s = jnp.einsum('bqhd,bkhd->bhqk', q, k).astype(jnp.float32) * (XA_HD ** -0.5)
    p = jax.nn.softmax(s, axis=-1).astype(v.dtype)
    o = jnp.einsum('bhqk,bkhd->bqhd', p, v).reshape(B, S, D_MODEL)
    return o @ wo


def conv_ffn(a, w_up, conv_w, conv_b, w_down):
    u = a @ w_up
    u = lax.conv_general_dilated(
        u, conv_w[:, None, :].astype(u.dtype), window_strides=(1,),
        padding=[(CONV_W - 1, 0)], dimension_numbers=('NWC', 'WIO', 'NWC'),
        feature_group_count=2 * D_FF) + conv_b
    gate, val = jnp.split(u, 2, axis=-1)
    return (jax.nn.silu(gate) * val) @ w_down


def setup_inputs(seed: int = 0) -> dict:
    key = jax.random.key(seed)
    ks = jax.random.split(key, 24)
    D = D_MODEL
    nrm = lambda k, shape, fan_in: jax.random.normal(k, shape, jnp.float32) * fan_in ** -0.5
    gain = lambda k, shape: 1.0 + 0.02 * jax.random.normal(k, shape, jnp.float32)
    b_i = 0.1 * jax.random.normal(ks[14], (N_B, ML_HEADS), jnp.float32)
    b_f = 3.0 + 0.5 * jax.random.normal(ks[15], (N_B, ML_HEADS), jnp.float32)
    return {
        "x": jax.random.normal(ks[0], (BATCH, SEQ, D), jnp.float32),
        "mem": jax.random.normal(ks[1], (BATCH, N_MEM, D), jnp.float32),
        "norm_mix_g": gain(ks[2], (DEPTH, D)),
        "norm_xa_g": gain(ks[3], (DEPTH, D)),
        "norm_mem_g": gain(ks[4], (DEPTH, D)),
        "norm_ffn_g": gain(ks[5], (DEPTH, D)),
        "hg_w_in": nrm(ks[6], (N_A, D, HG_IN), D),
        "hg_w_out": nrm(ks[7], (N_A, D, D), D),
        "hg_norm_g": gain(ks[8], (N_A, D)),
        "hg_lb_logits": 0.5 * jax.random.normal(ks[9], (DEPTH, HG_HEADS * HG_DK), jnp.float32),
        "ml_w_in": nrm(ks[10], (N_B, D, ML_IN), D),
        "ml_b_gate": jnp.concatenate([b_i, b_f], axis=-1),
        "ml_w_out": nrm(ks[11], (N_B, D, D), D),
        "ml_norm_g": gain(ks[12], (N_B, D)),
        "xa_wq": nrm(ks[13], (DEPTH, D, D), D),
        "xa_wkv": nrm(ks[16], (DEPTH, D, 2 * D), D),
        "xa_wo": nrm(ks[17], (DEPTH, D, D), D),
        "ffn_w_up": nrm(ks[18], (DEPTH, D, 2 * D_FF), D),
        "ffn_conv_w": nrm(ks[19], (DEPTH, CONV_W, 2 * D_FF), CONV_W),
        "ffn_conv_b": 0.02 * jax.random.normal(ks[20], (DEPTH, 2 * D_FF), jnp.float32),
        "ffn_w_down": nrm(ks[21], (DEPTH, D_FF, D), D_FF),
        "final_g": gain(ks[22], (D,)),
    }


def reference(x, mem, norm_mix_g, norm_xa_g, norm_mem_g, norm_ffn_g,
              hg_w_in, hg_w_out, hg_norm_g, hg_lb_logits,
              ml_w_in, ml_b_gate, ml_w_out, ml_norm_g,
              xa_wq, xa_wkv, xa_wo,
              ffn_w_up, ffn_conv_w, ffn_conv_b, ffn_w_down, final_g):
    p = jax.nn.softmax(hg_lb_logits.astype(jnp.float32), axis=0)
    lower_bounds = jnp.cumsum(p, axis=0) - p[0]
    h = x
    for layer in range(DEPTH):
        j = layer // N_MIXERS
        a = rmsnorm(h, norm_mix_g[layer])
        if layer % N_MIXERS == 0:
            h = h + hgrn2_mixer(a, hg_w_in[j], hg_w_out[j], hg_norm_g[j],
                                lower_bounds[layer])
        else:
            h = h + mlstm_mixer(a, ml_w_in[j], ml_b_gate[j], ml_w_out[j], ml_norm_g[j])
        a = rmsnorm(h, norm_xa_g[layer])
        memn = rmsnorm(mem, norm_mem_g[layer])
        h = h + memory_cross_attn(a, memn, xa_wq[layer], xa_wkv[layer], xa_wo[layer])
        a = rmsnorm(h, norm_ffn_g[layer])
        h = h + conv_ffn(a, ffn_w_up[layer], ffn_conv_w[layer], ffn_conv_b[layer],
                         ffn_w_down[layer])
    return rmsnorm(h, final_g)
```

```python
import functools

import jax
import jax.numpy as jnp
from jax import lax
from jax.experimental import pallas as pl
from jax.experimental.pallas import tpu as pltpu

F32 = jnp.float32
BF16 = jnp.bfloat16

EPS = 1e-6
NEG_BIG = -1e30
F_FLOOR = 1e-20
CHUNK = 64
SUB = 16
HG_HEADS = 8
HG_D = 128
ML_HEADS = 8
ML_DQK = 64
ML_DV = 128
XA_HEADS = 4
CONV_W = 3
HALO = 16

VMEM_LIMIT = 56 * 1024 * 1024


def _cparams(*sem):
    return pltpu.CompilerParams(dimension_semantics=sem, vmem_limit_bytes=VMEM_LIMIT)


def _dot(a, b):
    return jnp.dot(a, b, preferred_element_type=F32)


def _dot_nt(a, b):
    return lax.dot_general(a, b, (((1,), (1,)), ((), ())), preferred_element_type=F32)


def _dot_tn(a, b):
    return lax.dot_general(a, b, (((0,), (0,)), ((), ())), preferred_element_type=F32)


def _dot_f32(a, b):
    return jnp.dot(a, b, preferred_element_type=F32, precision=lax.Precision.HIGHEST)


def _sigmoid(x):
    return 1.0 / (1.0 + jnp.exp(-x))


def _rms_rows(x, g):
    ms = jnp.mean(x * x, axis=-1, keepdims=True)
    return x * lax.rsqrt(ms + EPS) * g


def _rmsnorm_kernel(x_ref, g_ref, o_ref):
    o_ref[...] = _rms_rows(x_ref[...], g_ref[...]).astype(o_ref.dtype)


def rmsnorm(x2d, g, out_dtype, tm):
    t, d = x2d.shape
    return pl.pallas_call(
        _rmsnorm_kernel,
        grid=(t // tm,),
        in_specs=[pl.BlockSpec((tm, d), lambda i: (i, 0)),
                  pl.BlockSpec((1, d), lambda i: (0, 0))],
        out_specs=pl.BlockSpec((tm, d), lambda i: (i, 0)),
        out_shape=jax.ShapeDtypeStruct((t, d), out_dtype),
        compiler_params=_cparams("parallel"),
        name="rmsnorm",
    )(x2d, g.reshape(1, d))


def _hg_inproj_kernel(layer, a_ref, wq_ref, wz_ref, wv_ref, wg_ref, lbl_ref,
                      q_ref, k_ref, lf_ref, v_ref, g_ref):
    a = a_ref[...]
    q = _dot(a, wq_ref[...])
    q_ref[...] = (q * _sigmoid(q)).astype(BF16)
    v_ref[...] = _dot(a, wv_ref[...]).astype(BF16)
    g = _dot(a, wg_ref[...])
    g_ref[...] = (g * _sigmoid(g)).astype(BF16)
    depth = lbl_ref.shape[0]
    rows = [lbl_ref[l:l + 1, :] for l in range(depth)]
    mx = functools.reduce(jnp.maximum, rows)
    es = [jnp.exp(r - mx) for r in rows]
    tot = functools.reduce(lambda x, y: x + y, es)
    ps = [e / tot for e in es]
    lb = functools.reduce(lambda x, y: x + y, ps[:layer + 1]) - ps[0]
    z = _dot(a, wz_ref[...])
    f = lb + (1.0 - lb) * _sigmoid(z)
    lf_ref[...] = jnp.log(jnp.maximum(f, F_FLOOR))
    k_ref[...] = ((1.0 - lb) * _sigmoid(-z)).astype(BF16)


def hg_inproj(a, w_in, lb_logits, layer, tm, tn):
    t, d = a.shape
    nj = d // tn
    wspec = lambda grp: pl.BlockSpec((d, tn), lambda i, j, grp=grp: (0, grp * nj + j))
    ospec = pl.BlockSpec((tm, tn), lambda i, j: (i, j))
    return pl.pallas_call(
        functools.partial(_hg_inproj_kernel, layer),
        grid=(t // tm, nj),
        in_specs=[pl.BlockSpec((tm, d), lambda i, j: (i, 0)),
                  wspec(0), wspec(1), wspec(2), wspec(3),
                  pl.BlockSpec((lb_logits.shape[0], tn), lambda i, j: (0, j))],
        out_specs=[ospec] * 5,
        out_shape=[jax.ShapeDtypeStruct((t, d), BF16),
                   jax.ShapeDtypeStruct((t, d), BF16),
                   jax.ShapeDtypeStruct((t, d), F32),
                   jax.ShapeDtypeStruct((t, d), BF16),
                   jax.ShapeDtypeStruct((t, d), BF16)],
        compiler_params=_cparams("parallel", "arbitrary"),
        name="hg_inproj",
    )(a, w_in, w_in, w_in, w_in, lb_logits)


def _hg_chunk_kernel(q_ref, k_ref, v_ref, lf_ref, g_ref, ng_ref, o_ref,
                     st_ref, b_s, k_s):
    lblk = q_ref.shape[0]
    n_sub = CHUNK // SUB

    @pl.when(pl.program_id(1) == 0)
    def _():
        st_ref[...] = jnp.zeros_like(st_ref)

    row = lax.broadcasted_iota(jnp.int32, (CHUNK, CHUNK), 0)
    col = lax.broadcasted_iota(jnp.int32, (CHUNK, CHUNK), 1)
    tril = (row >= col).astype(F32)
    sub_row = lax.broadcasted_iota(jnp.int32, (SUB, HG_D), 0)
    chunk_row = lax.broadcasted_iota(jnp.int32, (CHUNK, HG_D), 0)
    a_lane = lax.broadcasted_iota(jnp.int32, (SUB, CHUNK), 1)

    def chunk_body(c, carry):
        r0 = pl.multiple_of(c * CHUNK, CHUNK)

        def head_body(h, carry2):
            c0 = pl.multiple_of(h * HG_D, HG_D)
            rs, cs = pl.ds(r0, CHUNK), pl.ds(c0, HG_D)
            qc = q_ref[rs, cs].astype(F32)
            kc = k_ref[rs, cs].astype(F32)
            vc = v_ref[rs, cs]
            b = _dot_f32(tril, lf_ref[rs, cs])
            b_s[...] = b
            k_s[...] = kc
            b_last = b[CHUNK - 1:CHUNK, :]
            st = st_ref[h]
            o_inter = _dot_nt((qc * jnp.exp(b)).astype(BF16), st.astype(BF16))
            k_dec = kc * jnp.exp(b_last - b)
            st_ref[h] = st * jnp.exp(b_last) + _dot_tn(vc, k_dec.astype(BF16))
            a_rows = []
            for i in range(n_sub):
                lo = i * SUB
                b_i = b[lo:lo + SUB, :]
                q_i = qc[lo:lo + SUB, :]
                a_i = jnp.zeros((SUB, CHUNK), F32)
                if i > 0:
                    ref_b = b[lo - 1:lo, :]
                    q_t = q_i * jnp.exp(b_i - ref_b)
                    k_t = kc * jnp.exp(jnp.where(chunk_row < lo, ref_b - b, NEG_BIG))
                    a_i = _dot_nt(q_t.astype(BF16), k_t.astype(BF16))
                for s in range(SUB):
                    b_row = b_s[pl.ds(lo + s, 1), :]
                    k_row = k_s[pl.ds(lo + s, 1), :]
                    e = jnp.exp(jnp.where(sub_row >= s, b_i - b_row, NEG_BIG))
                    a_col = jnp.sum(e * q_i * k_row, axis=-1, keepdims=True)
                    a_i = jnp.where(a_lane == lo + s, a_col, a_i)
                a_rows.append(a_i)
            a_mat = jnp.concatenate(a_rows, axis=0)
            o = o_inter + _dot(a_mat.astype(BF16), vc)
            ms = jnp.mean(o * o, axis=-1, keepdims=True)
            o = o * lax.rsqrt(ms + EPS) * ng_ref[:, cs] * g_ref[rs, cs].astype(F32)
            o_ref[rs, cs] = o.astype(o_ref.dtype)
            return carry2

        return lax.fori_loop(0, HG_HEADS, head_body, carry)

    lax.fori_loop(0, lblk // CHUNK, chunk_body, 0)


def hg_chunk(q, k, v, lf, g, norm_g, lblk):
    bsz, s, d = q.shape
    spec = pl.BlockSpec((None, lblk, d), lambda b, c: (b, c, 0))
    return pl.pallas_call(
        _hg_chunk_kernel,
        grid=(bsz, s // lblk),
        in_specs=[spec, spec, spec, spec, spec, pl.BlockSpec((1, d), lambda b, c: (0, 0))],
        out_specs=spec,
        out_shape=jax.ShapeDtypeStruct((bsz, s, d), BF16),
        scratch_shapes=[pltpu.VMEM((HG_HEADS, HG_D, HG_D), F32),
                        pltpu.VMEM((CHUNK, HG_D), F32),
                        pltpu.VMEM((CHUNK, HG_D), F32)],
        compiler_params=_cparams("parallel", "arbitrary"),
        name="hg_chunk",
    )(q, k, v, lf, g, norm_g.reshape(1, d))


def _ml_inproj_kernel(n_lin, a_ref, w_ref, wg_ref, bg_ref, p_ref, gt_ref):
    j = pl.program_id(1)
    a = a_ref[...]
    acc = _dot(a, w_ref[...])

    @pl.when(j < n_lin)
    def _():
        p_ref[...] = acc.astype(BF16)

    @pl.when(j >= n_lin)
    def _():
        p_ref[...] = _sigmoid(acc).astype(BF16)

    @pl.when(j == 0)
    def _():
        gts = _dot(a, wg_ref[...]) + bg_ref[...]
        lane = lax.broadcasted_iota(jnp.int32, gts.shape, 1)
        log_sig = jnp.minimum(gts, 0.0) - jnp.log(1.0 + jnp.exp(-jnp.abs(gts)))
        gt_ref[...] = jnp.where(lane < ML_HEADS, gts, log_sig)


def ml_inproj(a, w_main, w_gate, b_gate, tm, tn, n_lin):
    t, d = a.shape
    n = w_main.shape[1]
    gl = w_gate.shape[1]
    return pl.pallas_call(
        functools.partial(_ml_inproj_kernel, n_lin),
        grid=(t // tm, n // tn),
        in_specs=[pl.BlockSpec((tm, d), lambda i, j: (i, 0)),
                  pl.BlockSpec((d, tn), lambda i, j: (0, j)),
                  pl.BlockSpec((d, gl), lambda i, j: (0, 0)),
                  pl.BlockSpec((1, gl), lambda i, j: (0, 0))],
        out_specs=[pl.BlockSpec((tm, tn), lambda i, j: (i, j)),
                   pl.BlockSpec((tm, gl), lambda i, j: (i, 0))],
        out_shape=[jax.ShapeDtypeStruct((t, n), BF16),
                   jax.ShapeDtypeStruct((t, gl), F32)],
        compiler_params=_cparams("parallel", "arbitrary"),
        name="ml_inproj",
    )(a, w_main, w_gate, b_gate)


def _ml_chunk_kernel(qk_ref, v_ref, og_ref, gc_ref, gr_ref, ng_ref, o_ref,
                     c_ref, m_ref):
    lblk = v_ref.shape[0]
    nqk = ML_HEADS * ML_DQK

    @pl.when(pl.program_id(1) == 0)
    def _():
        c_ref[...] = jnp.zeros_like(c_ref)
        m_ref[...] = jnp.zeros_like(m_ref)

    row = lax.broadcasted_iota(jnp.int32, (CHUNK, CHUNK), 0)
    col = lax.broadcasted_iota(jnp.int32, (CHUNK, CHUNK), 1)
    causal = row >= col
    tril = causal.astype(F32)
    triu = (row <= col).astype(F32)
    ones_v = jnp.ones((CHUNK, ML_DV), BF16)
    scale = ML_DQK ** -0.5

    def chunk_body(c, carry):
        r0 = pl.multiple_of(c * CHUNK, CHUNK)
        rs = pl.ds(r0, CHUNK)
        g_cols = gc_ref[rs, :]
        g_rows = gr_ref[c]
        b_cols = _dot_f32(tril, g_cols)
        b_rows = _dot_f32(g_rows, triu)
        for h in range(ML_HEADS):
            q = qk_ref[rs, h * ML_DQK:(h + 1) * ML_DQK]
            k = qk_ref[rs, nqk + h * ML_DQK:nqk + (h + 1) * ML_DQK]
            v = v_ref[rs, h * ML_DV:(h + 1) * ML_DV]
            v_ext = jnp.concatenate([v, ones_v], axis=1)
            b_col = b_cols[:, ML_HEADS + h:ML_HEADS + h + 1]
            li_col = g_cols[:, h:h + 1]
            b_row = b_rows[ML_HEADS + h:ML_HEADS + h + 1, :]
            li_row = g_rows[h:h + 1, :]
            m_prev = m_ref[h:h + 1, 0:1]
            c_prev = c_ref[h]
            log_d = jnp.where(causal, b_col - b_row + li_row, NEG_BIG)
            inter = b_col + m_prev
            m_t = jnp.maximum(inter, jnp.max(log_d, axis=-1, keepdims=True))
            d_m = jnp.exp(log_d - m_t)
            w_inter = jnp.exp(inter - m_t)
            s_mat = _dot_nt(q, k) * scale * d_m
            num = (_dot(s_mat.astype(BF16), v_ext)
                   + w_inter * _dot(q, c_prev.astype(BF16)))
            den = num[:, ML_DV:]
            hh = num[:, :ML_DV] / jnp.maximum(jnp.abs(den), jnp.exp(-m_t))
            b_last = b_col[CHUNK - 1:CHUNK, :]
            log_w = b_last - b_col + li_col
            m_new = jnp.maximum(b_last + m_prev, jnp.max(log_w, axis=0, keepdims=True))
            w = jnp.exp(log_w - m_new)
            dec = jnp.exp(b_last + m_prev - m_new)
            kw = (k.astype(F32) * (w * scale)).astype(BF16)
            c_ref[h] = dec * c_prev + _dot_tn(kw, v_ext)
            m_ref[h:h + 1, :] = jnp.broadcast_to(m_new, (1, m_ref.shape[1]))
            vs = pl.ds(h * ML_DV, ML_DV)
            ms = jnp.mean(hh * hh, axis=-1, keepdims=True)
            out = hh * lax.rsqrt(ms + EPS) * ng_ref[:, vs] * og_ref[rs, vs].astype(F32)
            o_ref[rs, vs] = out.astype(o_ref.dtype)
        return carry

    lax.fori_loop(0, lblk // CHUNK, chunk_body, 0)


def ml_chunk(proj, gates_cols, gates_rows, norm_g, lblk):
    bsz, s, _ = proj.shape
    d = ML_HEADS * ML_DV
    nqk2 = 2 * ML_HEADS * ML_DQK
    gl = gates_cols.shape[-1]
    return pl.pallas_call(
        _ml_chunk_kernel,
        grid=(bsz, s // lblk),
        in_specs=[pl.BlockSpec((None, lblk, nqk2), lambda b, c: (b, c, 0)),
                  pl.BlockSpec((None, lblk, d), lambda b, c: (b, c, nqk2 // d)),
                  pl.BlockSpec((None, lblk, d), lambda b, c: (b, c, nqk2 // d + 1)),
                  pl.BlockSpec((None, lblk, gl), lambda b, c: (b, c, 0)),
                  pl.BlockSpec((None, lblk // CHUNK, 2 * ML_HEADS, CHUNK),
                               lambda b, c: (b, c, 0, 0)),
                  pl.BlockSpec((1, d), lambda b, c: (0, 0))],
        out_specs=pl.BlockSpec((None, lblk, d), lambda b, c: (b, c, 0)),
        out_shape=jax.ShapeDtypeStruct((bsz, s, d), BF16),
        scratch_shapes=[pltpu.VMEM((ML_HEADS, ML_DQK, 2 * ML_DV), F32),
                        pltpu.VMEM((ML_HEADS, 128), F32)],
        compiler_params=_cparams("parallel", "arbitrary"),
        name="ml_chunk",
    )(proj, proj, proj, gates_cols, gates_rows, norm_g.reshape(1, d))


def _outproj_kernel(o_ref, w_ref, h_ref, g_ref, hn_ref, an_ref):
    hn = h_ref[...] + _dot(o_ref[...], w_ref[...])
    hn_ref[...] = hn
    an_ref[...] = _rms_rows(hn, g_ref[...]).astype(an_ref.dtype)


def outproj_res_norm(o, w, h, g_next, a_dtype, tm):
    t, d = h.shape
    return pl.pallas_call(
        _outproj_kernel,
        grid=(t // tm,),
        in_specs=[pl.BlockSpec((tm, d), lambda i: (i, 0)),
                  pl.BlockSpec((d, d), lambda i: (0, 0)),
                  pl.BlockSpec((tm, d), lambda i: (i, 0)),
                  pl.BlockSpec((1, d), lambda i: (0, 0))],
        out_specs=[pl.BlockSpec((tm, d), lambda i: (i, 0))] * 2,
        out_shape=[jax.ShapeDtypeStruct((t, d), F32),
                   jax.ShapeDtypeStruct((t, d), a_dtype)],
        compiler_params=_cparams("parallel"),
        name="outproj_res_norm",
    )(o, w, h, g_next.reshape(1, d))


def _mem_kv_kernel(m_ref, g_ref, w_ref, kv_ref):
    mn = _rms_rows(m_ref[...], g_ref[...]).astype(BF16)
    kv_ref[...] = _dot(mn, w_ref[...]).astype(BF16)


def mem_kv(mem2d, g, wkv, tn):
    t, d = mem2d.shape
    n = wkv.shape[1]
    return pl.pallas_call(
        _mem_kv_kernel,
        grid=(n // tn,),
        in_specs=[pl.BlockSpec((t, d), lambda j: (0, 0)),
                  pl.BlockSpec((1, d), lambda j: (0, 0)),
                  pl.BlockSpec((d, tn), lambda j: (0, j))],
        out_specs=pl.BlockSpec((t, tn), lambda j: (0, j)),
        out_shape=jax.ShapeDtypeStruct((t, n), BF16),
        compiler_params=_cparams("parallel"),
        name="mem_kv",
    )(mem2d, g.reshape(1, d), wkv)


def _xattn_kernel(a_ref, k_ref, v_ref, wq_ref, wo_ref, h_ref, g_ref, hn_ref, an_ref):
    d = a_ref.shape[1]
    hd = d // XA_HEADS
    scale = hd ** -0.5
    q = _dot(a_ref[...], wq_ref[...]).astype(BF16)
    outs = []
    for hh in range(XA_HEADS):
        cs = slice(hh * hd, (hh + 1) * hd)
        s = _dot_nt(q[:, cs], k_ref[:, cs]) * scale
        p = jnp.exp(s - jnp.max(s, axis=-1, keepdims=True))
        p = p / jnp.sum(p, axis=-1, keepdims=True)
        outs.append(_dot(p.astype(BF16), v_ref[:, cs]).astype(BF16))
    o = jnp.concatenate(outs, axis=1)
    hn = h_ref[...] + _dot(o, wo_ref[...])
    hn_ref[...] = hn
    an_ref[...] = _rms_rows(hn, g_ref[...]).astype(an_ref.dtype)


def xattn_res_norm(a, kv, wq, wo, h, g_next, tm):
    bsz, s, d = a.shape
    n_mem = kv.shape[1]
    tok = pl.BlockSpec((None, tm, d), lambda b, i: (b, i, 0))
    wsp = pl.BlockSpec((d, d), lambda b, i: (0, 0))
    return pl.pallas_call(
        _xattn_kernel,
        grid=(bsz, s // tm),
        in_specs=[tok,
                  pl.BlockSpec((None, n_mem, d), lambda b, i: (b, 0, 0)),
                  pl.BlockSpec((None, n_mem, d), lambda b, i: (b, 0, 1)),
                  wsp, wsp, tok,
                  pl.BlockSpec((1, d), lambda b, i: (0, 0))],
        out_specs=[tok, tok],
        out_shape=[jax.ShapeDtypeStruct((bsz, s, d), F32),
                   jax.ShapeDtypeStruct((bsz, s, d), BF16)],
        compiler_params=_cparams("parallel", "parallel"),
        name="xattn_res_norm",
    )(a, kv, kv, wq, wo, h, g_next.reshape(1, d))


def _ffn_kernel(a_ref, ah_ref, wg_ref, wv_ref, cwg_ref, cwv_ref, cbg_ref, cbv_ref,
                wd_ref, h_ref, g_ref, hn_ref, an_ref, acc_ref):
    i = pl.program_id(1)
    j = pl.program_id(2)
    tm = a_ref.shape[0]

    @pl.when(j == 0)
    def _():
        acc_ref[...] = jnp.zeros_like(acc_ref)

    halo = jnp.where(i > 0, ah_ref[...], jnp.zeros_like(ah_ref))
    a_ext = jnp.concatenate([halo, a_ref[...]], axis=0)

    def conv(w_ref, cw_ref, cb_ref):
        u = _dot(a_ext, w_ref[...])
        out = cb_ref[...] + u[HALO:, :] * cw_ref[CONV_W - 1:CONV_W, :]
        for tap in range(CONV_W - 1):
            sh = CONV_W - 1 - tap
            out = out + u[HALO - sh:HALO - sh + tm, :] * cw_ref[tap:tap + 1, :]
        return out

    gate = conv(wg_ref, cwg_ref, cbg_ref)
    val = conv(wv_ref, cwv_ref, cbv_ref)
    y = (gate * _sigmoid(gate) * val).astype(BF16)
    acc_ref[...] += _dot(y, wd_ref[...])

    @pl.when(j == pl.num_programs(2) - 1)
    def _():
        hn = h_ref[...] + acc_ref[...]
        hn_ref[...] = hn
        an_ref[...] = _rms_rows(hn, g_ref[...]).astype(an_ref.dtype)


def ffn_res_norm(a, w_up, conv_w, conv_b, w_down, h, g_next, a_dtype, tm, tf):
    bsz, s, d = a.shape
    dff = w_down.shape[0]
    nf = dff // tf
    hb = tm // HALO
    tok = pl.BlockSpec((None, tm, d), lambda b, i, j: (b, i, 0))
    return pl.pallas_call(
        _ffn_kernel,
        grid=(bsz, s // tm, nf),
        in_specs=[tok,
                  pl.BlockSpec((None, HALO, d),
                               lambda b, i, j: (b, jnp.maximum(i * hb - 1, 0), 0)),
                  pl.BlockSpec((d, tf), lambda b, i, j: (0, j)),
                  pl.BlockSpec((d, tf), lambda b, i, j: (0, nf + j)),
                  pl.BlockSpec((CONV_W, tf), lambda b, i, j: (0, j)),
                  pl.BlockSpec((CONV_W, tf), lambda b, i, j: (0, nf + j)),
                  pl.BlockSpec((1, tf), lambda b, i, j: (0, j)),
                  pl.BlockSpec((1, tf), lambda b, i, j: (0, nf + j)),
                  pl.BlockSpec((tf, d), lambda b, i, j: (j, 0)),
                  tok,
                  pl.BlockSpec((1, d), lambda b, i, j: (0, 0))],
        out_specs=[tok, tok],
        out_shape=[jax.ShapeDtypeStruct((bsz, s, d), F32),
                   jax.ShapeDtypeStruct((bsz, s, d), a_dtype)],
        scratch_shapes=[pltpu.VMEM((tm, d), F32)],
        compiler_params=_cparams("parallel", "parallel", "arbitrary"),
        name="ffn_res_norm",
    )(a, a, w_up, w_up, conv_w, conv_w, conv_b.reshape(1, -1), conv_b.reshape(1, -1),
      w_down, h, g_next.reshape(1, d))


def _pick(n, pref):
    for c in pref:
        if n % c == 0:
            return c
    return n


def kernel(x, mem, norm_mix_g, norm_xa_g, norm_mem_g, norm_ffn_g, hg_w_in, hg_w_out, hg_norm_g, hg_lb_logits, ml_w_in, ml_b_gate, ml_w_out, ml_norm_g, xa_wq, xa_wkv, xa_wo, ffn_w_up, ffn_conv_w, ffn_conv_b, ffn_w_down, final_g):
    bsz, s, d = x.shape
    n_mem = mem.shape[1]
    depth = norm_mix_g.shape[0]
    t = bsz * s
    tm = _pick(s, (1024, 512, 256, 128, 64))
    lblk = _pick(s, (512, 256, 128, 64))
    tn = 256
    dff = ffn_w_down.shape[1]
    tf = _pick(dff, (256, 128))
    nqk2 = 2 * ML_HEADS * ML_DQK
    n_main = nqk2 + 2 * d
    gl = 128

    h = x.reshape(t, d)
    a = rmsnorm(h, norm_mix_g[0], BF16, tm)
    mem2d = mem.reshape(bsz * n_mem, d)

    for layer in range(depth):
        j = layer // 2
        if layer % 2 == 0:
            q, k, lf, v, g = hg_inproj(a, hg_w_in[j].astype(BF16), hg_lb_logits, layer, tm, tn)
            r3 = lambda z: z.reshape(bsz, s, d)
            o = hg_chunk(r3(q), r3(k), r3(v), r3(lf), r3(g), hg_norm_g[j], lblk)
            w_out = hg_w_out[j]
        else:
            w_in = ml_w_in[j]
            w_gate = jnp.pad(w_in[:, n_main:], ((0, 0), (0, gl - 2 * ML_HEADS))).astype(BF16)
            b_gate = jnp.pad(ml_b_gate[j], (0, gl - 2 * ML_HEADS)).reshape(1, gl)
            proj, gts = ml_inproj(a, w_in[:, :n_main].astype(BF16), w_gate, b_gate,
                                  tm, tn, (nqk2 + d) // tn)
            gts = gts.reshape(bsz, s, gl)
            g_rows = gts[:, :, :2 * ML_HEADS].reshape(bsz, s // CHUNK, CHUNK, 2 * ML_HEADS)
            g_rows = g_rows.transpose(0, 1, 3, 2)
            o = ml_chunk(proj.reshape(bsz, s, n_main), gts, g_rows, ml_norm_g[j], lblk)
            w_out = ml_w_out[j]
        h, a = outproj_res_norm(o.reshape(t, d), w_out.astype(BF16), h, norm_xa_g[layer], BF16, tm)

        kv = mem_kv(mem2d, norm_mem_g[layer], xa_wkv[layer].astype(BF16), 512)
        h3, a3 = xattn_res_norm(a.reshape(bsz, s, d), kv.reshape(bsz, n_mem, 2 * d),
                                xa_wq[layer].astype(BF16), xa_wo[layer].astype(BF16),
                                h.reshape(bsz, s, d), norm_ffn_g[layer], min(tm, 512))

        last = layer == depth - 1
        g_next = final_g if last else norm_mix_g[layer + 1]
        h3, a3 = ffn_res_norm(a3, ffn_w_up[layer].astype(BF16), ffn_conv_w[layer],
                              ffn_conv_b[layer], ffn_w_down[layer].astype(BF16), h3, g_next,
                              F32 if last else BF16, tm, tf)
        h, a = h3.reshape(t, d), a3.reshape(t, d)

    return a.reshape(bsz, s, d)
```

```python
import functools

import jax
import jax.numpy as jnp
from jax import lax
from jax.experimental import pallas as pl
from jax.experimental.pallas import tpu as pltpu

F32 = jnp.float32
BF16 = jnp.bfloat16

EPS = 1e-6
NEG_BIG = -1e30
F_FLOOR = 1e-20
LOG2E = 1.4426950408889634
LANES = 128
CHUNK = 64
SUB = 16
HG_HEADS = 8
HG_D = 128
ML_HEADS = 8
ML_DQK = 64
ML_DV = 128
XA_HEADS = 4
CONV_W = 3
HALO = 16

VMEM_LIMIT = 56 * 1024 * 1024


def _cparams(*sem):
    return pltpu.CompilerParams(dimension_semantics=sem, vmem_limit_bytes=VMEM_LIMIT)


def _dot(a, b):
    return jnp.dot(a, b, preferred_element_type=F32)


def _dot_nt(a, b):
    return lax.dot_general(a, b, (((1,), (1,)), ((), ())), preferred_element_type=F32)


def _dot_tn(a, b):
    return lax.dot_general(a, b, (((0,), (0,)), ((), ())), preferred_element_type=F32)


def _dot_f32(a, b):
    return jnp.dot(a, b, preferred_element_type=F32, precision=lax.Precision.HIGHEST)


def _sigmoid(x):
    return 1.0 / (1.0 + jnp.exp(-x))


def _rms_rows(x, g):
    ms = jnp.mean(x * x, axis=-1, keepdims=True)
    return x * lax.rsqrt(ms + EPS) * g


def _rmsnorm_kernel(x_ref, g_ref, o_ref):
    o_ref[...] = _rms_rows(x_ref[...], g_ref[...]).astype(o_ref.dtype)


def rmsnorm(x2d, g, out_dtype, tm):
    t, d = x2d.shape
    return pl.pallas_call(
        _rmsnorm_kernel,
        grid=(t // tm,),
        in_specs=[pl.BlockSpec((tm, d), lambda i: (i, 0)),
                  pl.BlockSpec((1, d), lambda i: (0, 0))],
        out_specs=pl.BlockSpec((tm, d), lambda i: (i, 0)),
        out_shape=jax.ShapeDtypeStruct((t, d), out_dtype),
        compiler_params=_cparams("parallel"),
        name="rmsnorm",
    )(x2d, g.reshape(1, d))


def _hg_inproj_kernel(layer, a_ref, wq_ref, wz_ref, wv_ref, wg_ref, lbl_ref,
                      q_ref, k_ref, lf_ref, v_ref, g_ref):
    a = a_ref[...]
    q = _dot(a, wq_ref[...])
    q_ref[...] = (q * _sigmoid(q)).astype(BF16)
    v_ref[...] = _dot(a, wv_ref[...]).astype(BF16)
    g = _dot(a, wg_ref[...])
    g_ref[...] = (g * _sigmoid(g)).astype(BF16)
    depth = lbl_ref.shape[0]
    rows = [lbl_ref[l:l + 1, :] for l in range(depth)]
    mx = functools.reduce(jnp.maximum, rows)
    es = [jnp.exp(r - mx) for r in rows]
    tot = functools.reduce(lambda x, y: x + y, es)
    ps = [e / tot for e in es]
    lb = functools.reduce(lambda x, y: x + y, ps[:layer + 1]) - ps[0]
    z = _dot(a, wz_ref[...])
    f = lb + (1.0 - lb) * _sigmoid(z)
    lf_ref[...] = jnp.log(jnp.maximum(f, F_FLOOR))
    k_ref[...] = ((1.0 - lb) * _sigmoid(-z)).astype(BF16)


def hg_inproj(a, w_in, lb_logits, layer, tm, tn):
    t, d = a.shape
    nj = d // tn
    wspec = lambda grp: pl.BlockSpec((d, tn), lambda i, j, grp=grp: (0, grp * nj + j))
    ospec = pl.BlockSpec((tm, tn), lambda i, j: (i, j))
    return pl.pallas_call(
        functools.partial(_hg_inproj_kernel, layer),
        grid=(t // tm, nj),
        in_specs=[pl.BlockSpec((tm, d), lambda i, j: (i, 0)),
                  wspec(0), wspec(1), wspec(2), wspec(3),
                  pl.BlockSpec((lb_logits.shape[0], tn), lambda i, j: (0, j))],
        out_specs=[ospec] * 5,
        out_shape=[jax.ShapeDtypeStruct((t, d), BF16),
                   jax.ShapeDtypeStruct((t, d), BF16),
                   jax.ShapeDtypeStruct((t, d), F32),
                   jax.ShapeDtypeStruct((t, d), BF16),
                   jax.ShapeDtypeStruct((t, d), BF16)],
        compiler_params=_cparams("parallel", "arbitrary"),
        name="hg_inproj",
    )(a, w_in, w_in, w_in, w_in, lb_logits)


def _hg_chunk_kernel(q_ref, k_ref, v_ref, lf_ref, g_ref, ng_ref, o_ref,
                     st_ref, b_s, q_s, k_s, v_s, o_s):
    lblk, d = q_ref.shape
    n_sub = CHUNK // SUB
    heads = [slice(h * HG_D, (h + 1) * HG_D) for h in range(d // HG_D)]

    @pl.when(pl.program_id(1) == 0)
    def _():
        st_ref[...] = jnp.zeros_like(st_ref)

    row = lax.broadcasted_iota(jnp.int32, (CHUNK, CHUNK), 0)
    col = lax.broadcasted_iota(jnp.int32, (CHUNK, CHUNK), 1)
    tril = (row >= col).astype(F32)
    chunk_row = lax.broadcasted_iota(jnp.int32, (CHUNK, d), 0)
    grp_row = lax.broadcasted_iota(jnp.int32, (8, 1), 0)

    def chunk_body(c, carry):
        rs = pl.ds(pl.multiple_of(c * CHUNK, CHUNK), CHUNK)
        b_s[...] = _dot_f32(tril, lf_ref[rs, :]) * LOG2E
        q_s[...] = q_ref[rs, :].astype(F32)
        k_s[...] = k_ref[rs, :].astype(F32)
        v_s[...] = v_ref[rs, :].astype(F32)
        vb = v_ref[rs, :]

        b = b_s[...]
        b_last = b_s[CHUNK - 1:CHUNK, :]
        qb = (q_s[...] * jnp.exp2(b)).astype(BF16)
        k_dec = (k_s[...] * jnp.exp2(b_last - b)).astype(BF16)
        dec = jnp.exp2(b_last)
        for h, cs in enumerate(heads):
            st = st_ref[h]
            o_s[:, cs] = _dot_nt(qb[:, cs], st.astype(BF16))
            st_ref[h] = st * dec[:, cs] + _dot_tn(vb[:, cs], k_dec[:, cs])

        a_off = [[] for _ in heads]
        for i in range(1, n_sub):
            lo = i * SUB
            ref_b = b_s[lo - 1:lo, :]
            q_t = (q_s[lo:lo + SUB, :] * jnp.exp2(b_s[lo:lo + SUB, :] - ref_b)).astype(BF16)
            k_t = (k_s[...] * jnp.exp2(jnp.where(chunk_row < lo, ref_b - b_s[...], NEG_BIG))
                   ).astype(BF16)
            for h, cs in enumerate(heads):
                a_off[h].append(_dot_nt(q_t[:, cs], k_t[:, cs]))
        for h, cs in enumerate(heads):
            a_mat = jnp.concatenate([jnp.zeros((SUB, CHUNK), F32)] + a_off[h], axis=0)
            o_s[:, cs] += _dot(a_mat.astype(BF16), vb[:, cs])

        for i in range(n_sub):
            lo = i * SUB
            acc = [[jnp.zeros((8, HG_D), F32) for _ in heads] for _ in range(SUB // 8)]
            for s in range(SUB):
                sr = pl.ds(lo + s, 1)
                for h, cs in enumerate(heads):
                    b_row, k_row, v_row = b_s[sr, cs], k_s[sr, cs], v_s[sr, cs]
                    for g in range(s // 8, SUB // 8):
                        ts = slice(lo + 8 * g, lo + 8 * g + 8)
                        p = jnp.exp2(b_s[ts, cs] - b_row) * (q_s[ts, cs] * k_row)
                        a = jnp.sum(p, axis=-1, keepdims=True)
                        if g == s // 8 and s % 8 > 0:
                            a = jnp.where(grp_row >= s % 8, a, 0.0)
                        acc[g][h] = acc[g][h] + a * v_row
            for g in range(SUB // 8):
                ts = slice(lo + 8 * g, lo + 8 * g + 8)
                for h, cs in enumerate(heads):
                    o_s[ts, cs] += acc[g][h]

        for h, cs in enumerate(heads):
            o = o_s[:, cs]
            ms = jnp.mean(o * o, axis=-1, keepdims=True)
            o = o * lax.rsqrt(ms + EPS) * ng_ref[:, cs] * g_ref[rs, cs].astype(F32)
            o_ref[rs, cs] = o.astype(o_ref.dtype)
        return carry

    lax.fori_loop(0, lblk // CHUNK, chunk_body, 0)


def hg_chunk(q, k, v, lf, g, norm_g, lblk):
    bsz, s, d = q.shape
    spec = pl.BlockSpec((None, lblk, d), lambda b, c: (b, c, 0))
    return pl.pallas_call(
        _hg_chunk_kernel,
        grid=(bsz, s // lblk),
        in_specs=[spec, spec, spec, spec, spec, pl.BlockSpec((1, d), lambda b, c: (0, 0))],
        out_specs=spec,
        out_shape=jax.ShapeDtypeStruct((bsz, s, d), BF16),
        scratch_shapes=[pltpu.VMEM((d // HG_D, HG_D, HG_D), F32)]
                       + [pltpu.VMEM((CHUNK, d), F32)] * 5,
        compiler_params=_cparams("parallel", "arbitrary"),
        name="hg_chunk",
    )(q, k, v, lf, g, norm_g.reshape(1, d))


def _ml_inproj_kernel(n_lin, a_ref, w_ref, wg_ref, bg_ref, p_ref, gt_ref):
    j = pl.program_id(1)
    a = a_ref[...]
    acc = _dot(a, w_ref[...])

    @pl.when(j < n_lin)
    def _():
        p_ref[...] = acc.astype(BF16)

    @pl.when(j >= n_lin)
    def _():
        p_ref[...] = _sigmoid(acc).astype(BF16)

    @pl.when(j == 0)
    def _():
        gts = _dot(a, wg_ref[...]) + bg_ref[...]
        lane = lax.broadcasted_iota(jnp.int32, gts.shape, 1)
        log_sig = jnp.minimum(gts, 0.0) - jnp.log(1.0 + jnp.exp(-jnp.abs(gts)))
        gt_ref[...] = jnp.where(lane < ML_HEADS, gts, log_sig)


def ml_inproj(a, w_main, w_gate, b_gate, tm, tn, n_lin):
    t, d = a.shape
    n = w_main.shape[1]
    gl = w_gate.shape[1]
    return pl.pallas_call(
        functools.partial(_ml_inproj_kernel, n_lin),
        grid=(t // tm, n // tn),
        in_specs=[pl.BlockSpec((tm, d), lambda i, j: (i, 0)),
                  pl.BlockSpec((d, tn), lambda i, j: (0, j)),
                  pl.BlockSpec((d, gl), lambda i, j: (0, 0)),
                  pl.BlockSpec((1, gl), lambda i, j: (0, 0))],
        out_specs=[pl.BlockSpec((tm, tn), lambda i, j: (i, j)),
                   pl.BlockSpec((tm, gl), lambda i, j: (i, 0))],
        out_shape=[jax.ShapeDtypeStruct((t, n), BF16),
                   jax.ShapeDtypeStruct((t, gl), F32)],
        compiler_params=_cparams("parallel", "arbitrary"),
        name="ml_inproj",
    )(a, w_main, w_gate, b_gate)


def _ml_chunk_kernel(qk_ref, v_ref, og_ref, gc_ref, gr_ref, ng_ref, o_ref,
                     c_ref, m_ref):
    lblk = v_ref.shape[0]
    nqk = ML_HEADS * ML_DQK

    @pl.when(pl.program_id(1) == 0)
    def _():
        c_ref[...] = jnp.zeros_like(c_ref)
        m_ref[...] = jnp.zeros_like(m_ref)

    row = lax.broadcasted_iota(jnp.int32, (CHUNK, CHUNK), 0)
    col = lax.broadcasted_iota(jnp.int32, (CHUNK, CHUNK), 1)
    causal = row >= col
    tril = causal.astype(F32)
    triu = (row <= col).astype(F32)
    ones_v = jnp.ones((CHUNK, ML_DV), BF16)
    scale = ML_DQK ** -0.5

    def chunk_body(c, carry):
        r0 = pl.multiple_of(c * CHUNK, CHUNK)
        rs = pl.ds(r0, CHUNK)
        g_cols = gc_ref[rs, :]
        g_rows = gr_ref[c]
        b_cols = _dot_f32(tril, g_cols)
        b_rows = _dot_f32(g_rows, triu)
        heads = range(ML_HEADS)
        qs = [qk_ref[rs, h * ML_DQK:(h + 1) * ML_DQK] for h in heads]
        ks = [qk_ref[rs, nqk + h * ML_DQK:nqk + (h + 1) * ML_DQK] for h in heads]
        v_exts = [jnp.concatenate([v_ref[rs, h * ML_DV:(h + 1) * ML_DV], ones_v], axis=1)
                  for h in heads]
        c_prevs = [c_ref[h] for h in heads]
        b_reps = [jnp.broadcast_to(b_cols[:, ML_HEADS + h:ML_HEADS + h + 1], (CHUNK, ML_DV))
                  for h in heads]
        li_reps = [jnp.broadcast_to(g_cols[:, h:h + 1], (CHUNK, ML_DV)) for h in heads]
        s_qk = [_dot_nt(qs[h], ks[h]) for h in heads]
        q_c = [_dot(qs[h], c_prevs[h].astype(BF16)) for h in heads]
        m_ts, d_ms, w_inters, kws, decs = [], [], [], [], []
        for h in heads:
            b_rep = b_reps[h]
            b_row = b_rows[ML_HEADS + h:ML_HEADS + h + 1, :]
            li_row = g_rows[h:h + 1, :]
            m_prev = m_ref[h:h + 1, :]
            log_d = jnp.where(causal, b_rep[:, :CHUNK] - b_row + li_row, NEG_BIG)
            inter = b_rep + m_prev
            m_t = jnp.maximum(inter, jnp.max(log_d, axis=-1, keepdims=True))
            m_ts.append(m_t)
            d_ms.append(jnp.exp(log_d - m_t[:, :CHUNK]))
            w_inters.append(jnp.exp(inter - m_t))
            b_last = b_rep[CHUNK - 1:CHUNK, :]
            log_w = b_last - b_rep + li_reps[h]
            m_new = jnp.maximum(b_last + m_prev, jnp.max(log_w, axis=0, keepdims=True))
            w = jnp.exp(log_w - m_new)
            decs.append(jnp.exp(b_last + m_prev - m_new))
            kws.append((ks[h].astype(F32) * (w[:, :ML_DQK] * scale)).astype(BF16))
            m_ref[h:h + 1, :] = m_new
        nums = []
        for h in heads:
            s_mat = (s_qk[h] * scale * d_ms[h]).astype(BF16)
            w2 = jnp.concatenate([w_inters[h], w_inters[h]], axis=1)
            nums.append(_dot(s_mat, v_exts[h]) + w2 * q_c[h])
            dec2 = jnp.concatenate([decs[h], decs[h]], axis=1)
            c_ref[h] = dec2 * c_prevs[h] + _dot_tn(kws[h], v_exts[h])
        for h in heads:
            den = nums[h][:, ML_DV:]
            hh = nums[h][:, :ML_DV] / jnp.maximum(jnp.abs(den), jnp.exp(-m_ts[h]))
            vs = pl.ds(h * ML_DV, ML_DV)
            ms = jnp.mean(hh * hh, axis=-1, keepdims=True)
            out = hh * lax.rsqrt(ms + EPS) * ng_ref[:, vs] * og_ref[rs, vs].astype(F32)
            o_ref[rs, vs] = out.astype(o_ref.dtype)
        return carry

    lax.fori_loop(0, lblk // CHUNK, chunk_body, 0)


def ml_chunk(proj, gates_cols, gates_rows, norm_g, lblk):
    bsz, s, _ = proj.shape
    d = ML_HEADS * ML_DV
    nqk2 = 2 * ML_HEADS * ML_DQK
    gl = gates_cols.shape[-1]
    return pl.pallas_call(
        _ml_chunk_kernel,
        grid=(bsz, s // lblk),
        in_specs=[pl.BlockSpec((None, lblk, nqk2), lambda b, c: (b, c, 0)),
                  pl.BlockSpec((None, lblk, d), lambda b, c: (b, c, nqk2 // d)),
                  pl.BlockSpec((None, lblk, d), lambda b, c: (b, c, nqk2 // d + 1)),
                  pl.BlockSpec((None, lblk, gl), lambda b, c: (b, c, 0)),
                  pl.BlockSpec((None, lblk // CHUNK, 2 * ML_HEADS, CHUNK),
                               lambda b, c: (b, c, 0, 0)),
                  pl.BlockSpec((1, d), lambda b, c: (0, 0))],
        out_specs=pl.BlockSpec((None, lblk, d), lambda b, c: (b, c, 0)),
        out_shape=jax.ShapeDtypeStruct((bsz, s, d), BF16),
        scratch_shapes=[pltpu.VMEM((ML_HEADS, ML_DQK, 2 * ML_DV), F32),
                        pltpu.VMEM((ML_HEADS, 128), F32)],
        compiler_params=_cparams("parallel", "arbitrary"),
        name="ml_chunk",
    )(proj, proj, proj, gates_cols, gates_rows, norm_g.reshape(1, d))


def _outproj_kernel(o_ref, w_ref, h_ref, g_ref, hn_ref, an_ref):
    hn = h_ref[...] + _dot(o_ref[...], w_ref[...])
    hn_ref[...] = hn
    an_ref[...] = _rms_rows(hn, g_ref[...]).astype(an_ref.dtype)


def outproj_res_norm(o, w, h, g_next, a_dtype, tm):
    t, d = h.shape
    return pl.pallas_call(
        _outproj_kernel,
        grid=(t // tm,),
        in_specs=[pl.BlockSpec((tm, d), lambda i: (i, 0)),
                  pl.BlockSpec((d, d), lambda i: (0, 0)),
                  pl.BlockSpec((tm, d), lambda i: (i, 0)),
                  pl.BlockSpec((1, d), lambda i: (0, 0))],
        out_specs=[pl.BlockSpec((tm, d), lambda i: (i, 0))] * 2,
        out_shape=[jax.ShapeDtypeStruct((t, d), F32),
                   jax.ShapeDtypeStruct((t, d), a_dtype)],
        compiler_params=_cparams("parallel"),
        name="outproj_res_norm",
    )(o, w, h, g_next.reshape(1, d))


def _mem_kv_kernel(m_ref, g_ref, w_ref, kv_ref):
    mn = _rms_rows(m_ref[...], g_ref[...]).astype(BF16)
    kv_ref[...] = _dot(mn, w_ref[...]).astype(BF16)


def mem_kv(mem2d, g, wkv, tn):
    t, d = mem2d.shape
    n = wkv.shape[1]
    return pl.pallas_call(
        _mem_kv_kernel,
        grid=(n // tn,),
        in_specs=[pl.BlockSpec((t, d), lambda j: (0, 0)),
                  pl.BlockSpec((1, d), lambda j: (0, 0)),
                  pl.BlockSpec((d, tn), lambda j: (0, j))],
        out_specs=pl.BlockSpec((t, tn), lambda j: (0, j)),
        out_shape=jax.ShapeDtypeStruct((t, n), BF16),
        compiler_params=_cparams("parallel"),
        name="mem_kv",
    )(mem2d, g.reshape(1, d), wkv)


def _xattn_kernel(a_ref, k_ref, v_ref, wq_ref, wo_ref, h_ref, g_ref, hn_ref, an_ref):
    d = a_ref.shape[1]
    hd = d // XA_HEADS
    scale = hd ** -0.5
    q = _dot(a_ref[...], wq_ref[...]).astype(BF16)
    outs = []
    for hh in range(XA_HEADS):
        cs = slice(hh * hd, (hh + 1) * hd)
        s = _dot_nt(q[:, cs], k_ref[:, cs]) * scale
        p = jnp.exp(s - jnp.max(s, axis=-1, keepdims=True))
        p = p / jnp.sum(p, axis=-1, keepdims=True)
        outs.append(_dot(p.astype(BF16), v_ref[:, cs]).astype(BF16))
    o = jnp.concatenate(outs, axis=1)
    hn = h_ref[...] + _dot(o, wo_ref[...])
    hn_ref[...] = hn
    an_ref[...] = _rms_rows(hn, g_ref[...]).astype(an_ref.dtype)


def xattn_res_norm(a, kv, wq, wo, h, g_next, tm):
    bsz, s, d = a.shape
    n_mem = kv.shape[1]
    tok = pl.BlockSpec((None, tm, d), lambda b, i: (b, i, 0))
    wsp = pl.BlockSpec((d, d), lambda b, i: (0, 0))
    return pl.pallas_call(
        _xattn_kernel,
        grid=(bsz, s // tm),
        in_specs=[tok,
                  pl.BlockSpec((None, n_mem, d), lambda b, i: (b, 0, 0)),
                  pl.BlockSpec((None, n_mem, d), lambda b, i: (b, 0, 1)),
                  wsp, wsp, tok,
                  pl.BlockSpec((1, d), lambda b, i: (0, 0))],
        out_specs=[tok, tok],
        out_shape=[jax.ShapeDtypeStruct((bsz, s, d), F32),
                   jax.ShapeDtypeStruct((bsz, s, d), BF16)],
        compiler_params=_cparams("parallel", "parallel"),
        name="xattn_res_norm",
    )(a, kv, kv, wq, wo, h, g_next.reshape(1, d))


def _ffn_kernel(tf, a_ref, ah_ref, wup_ref, cw_ref, cb_ref, wd_ref, h_ref, g_ref,
                hn_ref, an_ref, u_s, y_s):
    i = pl.program_id(1)
    tm = a_ref.shape[0]
    dff = wd_ref.shape[0]
    n_lane = tf // LANES

    halo = jnp.where(i > 0, ah_ref[...], jnp.zeros_like(ah_ref))
    a_ext = jnp.concatenate([halo, a_ref[...]], axis=0)

    for j in range(dff // tf):
        buf = j % 2
        for part in range(2):
            col = part * dff + j * tf
            u = _dot(a_ext, wup_ref[:, col:col + tf])
            for c in range(n_lane):
                u_s[buf, part, c] = u[:, c * LANES:(c + 1) * LANES]
        for c in range(n_lane):
            conv = []
            for part in range(2):
                lanes = slice(part * dff + j * tf + c * LANES,
                              part * dff + j * tf + (c + 1) * LANES)
                out = cb_ref[:, lanes]
                for tap in range(CONV_W):
                    sh = CONV_W - 1 - tap
                    out = out + (u_s[buf, part, c, HALO - sh:HALO - sh + tm, :]
                                 * cw_ref[tap:tap + 1, lanes])
                conv.append(out)
            gate, val = conv
            y_s[:, j * tf + c * LANES:j * tf + (c + 1) * LANES] = (
                gate * _sigmoid(gate) * val).astype(BF16)

    hn = h_ref[...] + _dot(y_s[...], wd_ref[...])
    hn_ref[...] = hn
    an_ref[...] = _rms_rows(hn, g_ref[...]).astype(an_ref.dtype)


def ffn_res_norm(a, w_up, conv_w, conv_b, w_down, h, g_next, a_dtype, tm, tf):
    bsz, s, d = a.shape
    dff = w_down.shape[0]
    hb = tm // HALO
    tok = pl.BlockSpec((None, tm, d), lambda b, i: (b, i, 0))
    whole = lambda arr: pl.BlockSpec(arr.shape, lambda b, i: (0, 0),
                                     pipeline_mode=pl.Buffered(1))
    conv_b = conv_b.reshape(1, -1)
    return pl.pallas_call(
        functools.partial(_ffn_kernel, tf),
        grid=(bsz, s // tm),
        in_specs=[tok,
                  pl.BlockSpec((None, HALO, d),
                               lambda b, i: (b, jnp.maximum(i * hb - 1, 0), 0)),
                  whole(w_up), whole(conv_w), whole(conv_b), whole(w_down),
                  tok,
                  pl.BlockSpec((1, d), lambda b, i: (0, 0))],
        out_specs=[tok, tok],
        out_shape=[jax.ShapeDtypeStruct((bsz, s, d), F32),
                   jax.ShapeDtypeStruct((bsz, s, d), a_dtype)],
        scratch_shapes=[pltpu.VMEM((2, 2, tf // LANES, HALO + tm, LANES), F32),
                        pltpu.VMEM((tm, dff), BF16)],
        compiler_params=_cparams("parallel", "parallel"),
        name="ffn_res_norm",
    )(a, a, w_up, conv_w, conv_b, w_down, h, g_next.reshape(1, d))


def _pick(n, pref):
    for c in pref:
        if n % c == 0:
            return c
    return n


def kernel(x, mem, norm_mix_g, norm_xa_g, norm_mem_g, norm_ffn_g, hg_w_in, hg_w_out, hg_norm_g, hg_lb_logits, ml_w_in, ml_b_gate, ml_w_out, ml_norm_g, xa_wq, xa_wkv, xa_wo, ffn_w_up, ffn_conv_w, ffn_conv_b, ffn_w_down, final_g):
    bsz, s, d = x.shape
    n_mem = mem.shape[1]
    depth = norm_mix_g.shape[0]
    t = bsz * s
    tm = _pick(s, (1024, 512, 256, 128, 64))
    lblk = _pick(s, (512, 256, 128, 64))
    tn = 256
    dff = ffn_w_down.shape[1]
    tf = _pick(dff, (256, 128))
    nqk2 = 2 * ML_HEADS * ML_DQK
    n_main = nqk2 + 2 * d
    gl = 128

    h = x.reshape(t, d)
    a = rmsnorm(h, norm_mix_g[0], BF16, tm)
    mem2d = mem.reshape(bsz * n_mem, d)

    for layer in range(depth):
        j = layer // 2
        if layer % 2 == 0:
            q, k, lf, v, g = hg_inproj(a, hg_w_in[j].astype(BF16), hg_lb_logits, layer, tm, tn)
            r3 = lambda z: z.reshape(bsz, s, d)
            o = hg_chunk(r3(q), r3(k), r3(v), r3(lf), r3(g), hg_norm_g[j], lblk)
            w_out = hg_w_out[j]
        else:
            w_in = ml_w_in[j]
            w_gate = jnp.pad(w_in[:, n_main:], ((0, 0), (0, gl - 2 * ML_HEADS))).astype(BF16)
            b_gate = jnp.pad(ml_b_gate[j], (0, gl - 2 * ML_HEADS)).reshape(1, gl)
            proj, gts = ml_inproj(a, w_in[:, :n_main].astype(BF16), w_gate, b_gate,
                                  tm, tn, (nqk2 + d) // tn)
            gts = gts.reshape(bsz, s, gl)
            g_rows = gts[:, :, :2 * ML_HEADS].reshape(bsz, s // CHUNK, CHUNK, 2 * ML_HEADS)
            g_rows = g_rows.transpose(0, 1, 3, 2)
            o = ml_chunk(proj.reshape(bsz, s, n_main), gts, g_rows, ml_norm_g[j], lblk)
            w_out = ml_w_out[j]
        h, a = outproj_res_norm(o.reshape(t, d), w_out.astype(BF16), h, norm_xa_g[layer], BF16, tm)

        kv = mem_kv(mem2d, norm_mem_g[layer], xa_wkv[layer].astype(BF16), 512)
        h3, a3 = xattn_res_norm(a.reshape(bsz, s, d), kv.reshape(bsz, n_mem, 2 * d),
                                xa_wq[layer].astype(BF16), xa_wo[layer].astype(BF16),
                                h.reshape(bsz, s, d), norm_ffn_g[layer], min(tm, 512))

        last = layer == depth - 1
        g_next = final_g if last else norm_mix_g[layer + 1]
        h3, a3 = ffn_res_norm(a3, ffn_w_up[layer].astype(BF16), ffn_conv_w[layer],
                              ffn_conv_b[layer], ffn_w_down[layer].astype(BF16), h3, g_next,
                              F32 if last else BF16, min(tm, 512), tf)
        h, a = h3.reshape(t, d), a3.reshape(t, d)

    return a.reshape(bsz, s, d)
```

```python
import functools

import jax
import jax.numpy as jnp
from jax import lax
from jax.experimental import pallas as pl
from jax.experimental.pallas import tpu as pltpu

F32 = jnp.float32
BF16 = jnp.bfloat16

EPS = 1e-6
NEG_BIG = -1e30
F_FLOOR = 1e-20
LOG2E = 1.4426950408889634
LANES = 128
CHUNK = 64
SUB = 16
HG_D = 128
HEAD_GROUP = 4
ML_HEADS = 8
ML_DQK = 64
ML_DV = 128
XA_HEADS = 4
CONV_W = 3
HALO = 16

VMEM_LIMIT = 56 * 1024 * 1024


def _cparams(*sem):
    return pltpu.CompilerParams(dimension_semantics=sem, vmem_limit_bytes=VMEM_LIMIT)


def _dot(a, b):
    return jnp.dot(a, b, preferred_element_type=F32)


def _dot_nt(a, b):
    return lax.dot_general(a, b, (((1,), (1,)), ((), ())), preferred_element_type=F32)


def _dot_tn(a, b):
    return lax.dot_general(a, b, (((0,), (0,)), ((), ())), preferred_element_type=F32)


def _dot_f32(a, b):
    return jnp.dot(a, b, preferred_element_type=F32, precision=lax.Precision.HIGHEST)


def _sigmoid(x):
    return 0.5 * jnp.tanh(0.5 * x) + 0.5


def _silu(x):
    return x * _sigmoid(x)


def _sigmoid_pair(x):
    e = jnp.exp(-jnp.abs(x))
    big = 1.0 / (1.0 + e)
    small = e * big
    pos = x >= 0.0
    return jnp.where(pos, big, small), jnp.where(pos, small, big)


def _whole(arr):
    nd = arr.ndim
    return pl.BlockSpec(arr.shape, lambda *_: (0,) * nd, pipeline_mode=pl.Buffered(1))


def _rms_rows(x, g):
    ms = jnp.mean(x * x, axis=-1, keepdims=True)
    return x * lax.rsqrt(ms + EPS) * g


def _rmsnorm_kernel(x_ref, g_ref, o_ref):
    o_ref[...] = _rms_rows(x_ref[...], g_ref[...]).astype(o_ref.dtype)


def rmsnorm(x2d, g, out_dtype, tm):
    t, d = x2d.shape
    return pl.pallas_call(
        _rmsnorm_kernel,
        grid=(t // tm,),
        in_specs=[pl.BlockSpec((tm, d), lambda i: (i, 0)),
                  pl.BlockSpec((1, d), lambda i: (0, 0))],
        out_specs=pl.BlockSpec((tm, d), lambda i: (i, 0)),
        out_shape=jax.ShapeDtypeStruct((t, d), out_dtype),
        compiler_params=_cparams("parallel"),
        name="rmsnorm",
    )(x2d, g.reshape(1, d))


def _hg_inproj_kernel(layer, tn, a_ref, w_ref, lbl_ref, q_ref, k_ref, lf_ref, v_ref, g_ref):
    a = a_ref[...]
    d = a.shape[1]
    depth = lbl_ref.shape[0]
    for j in range(d // tn):
        cols = slice(j * tn, (j + 1) * tn)
        proj = lambda grp: _dot(a, w_ref[:, grp * d + j * tn:grp * d + (j + 1) * tn])
        q_ref[:, cols] = _silu(proj(0)).astype(BF16)
        v_ref[:, cols] = proj(2).astype(BF16)
        g_ref[:, cols] = _silu(proj(3)).astype(BF16)
        rows = [lbl_ref[l:l + 1, cols] for l in range(depth)]
        mx = functools.reduce(jnp.maximum, rows)
        es = [jnp.exp(r - mx) for r in rows]
        tot = functools.reduce(lambda x, y: x + y, es)
        ps = [e / tot for e in es]
        lb = functools.reduce(lambda x, y: x + y, ps[:layer + 1]) - ps[0]
        sig_pos, sig_neg = _sigmoid_pair(proj(1))
        f = lb + (1.0 - lb) * sig_pos
        lf_ref[:, cols] = jnp.log(jnp.maximum(f, F_FLOOR))
        k_ref[:, cols] = ((1.0 - lb) * sig_neg).astype(BF16)


def hg_inproj(a, w_in, lb_logits, layer, tm, tn):
    t, d = a.shape
    tok = pl.BlockSpec((tm, d), lambda i: (i, 0))
    return pl.pallas_call(
        functools.partial(_hg_inproj_kernel, layer, tn),
        grid=(t // tm,),
        in_specs=[tok, _whole(w_in), _whole(lb_logits)],
        out_specs=[tok] * 5,
        out_shape=[jax.ShapeDtypeStruct((t, d), BF16),
                   jax.ShapeDtypeStruct((t, d), BF16),
                   jax.ShapeDtypeStruct((t, d), F32),
                   jax.ShapeDtypeStruct((t, d), BF16),
                   jax.ShapeDtypeStruct((t, d), BF16)],
        compiler_params=_cparams("parallel"),
        name="hg_inproj",
    )(a, w_in, lb_logits)


def _hg_chunk_kernel(q_ref, k_ref, v_ref, lf_ref, g_ref, ng_ref, o_ref,
                     st_ref, b_buf, q_buf, k_buf, o_buf, aoff_buf, amat_buf):
    lblk, d = q_ref.shape
    n_chunks = lblk // CHUNK
    n_sub = CHUNK // SUB
    heads = [slice(h * HG_D, (h + 1) * HG_D) for h in range(d // HG_D)]

    @pl.when(pl.program_id(1) == 0)
    def _():
        st_ref[...] = jnp.zeros_like(st_ref)

    row = lax.broadcasted_iota(jnp.int32, (CHUNK, CHUNK), 0)
    col = lax.broadcasted_iota(jnp.int32, (CHUNK, CHUNK), 1)
    tril = (row >= col).astype(F32)
    grp_row = lax.broadcasted_iota(jnp.int32, (8, CHUNK), 0)
    grp_lane = lax.broadcasted_iota(jnp.int32, (8, CHUNK), 1)

    def rows(c):
        return pl.ds(pl.multiple_of(c * CHUNK, CHUNK), CHUNK)


    def front(c):
        rs, par = rows(c), c % 2
        b_s, q_s, k_s, o_s = b_buf.at[par], q_buf.at[par], k_buf.at[par], o_buf.at[c % 3]
        b_s[...] = _dot_f32(tril, lf_ref[rs, :]) * LOG2E
        q_s[...] = q_ref[rs, :].astype(F32)
        k_s[...] = k_ref[rs, :].astype(F32)
        vb = v_ref[rs, :]
        b = b_s[...]
        b_last = b_s[CHUNK - 1:CHUNK, :]
        qb = (q_s[...] * jnp.exp2(b)).astype(BF16)
        k_dec = (k_s[...] * jnp.exp2(b_last - b)).astype(BF16)
        dec = jnp.exp2(b_last)
        for h, cs in enumerate(heads):
            st = st_ref[h]
            o_s[:, cs] = _dot_nt(qb[:, cs], st.astype(BF16))
            st_ref[h] = st * dec[:, cs] + _dot_tn(vb[:, cs], k_dec[:, cs])
        for h in range(len(heads)):
            aoff_buf[par, h, 0:SUB, :] = jnp.zeros((SUB, CHUNK), F32)
        for i in range(1, n_sub):
            lo = i * SUB
            ref_b = b_s[lo - 1:lo, :]
            q_t = (q_s[lo:lo + SUB, :] * jnp.exp2(b_s[lo:lo + SUB, :] - ref_b)).astype(BF16)
            k_t = jnp.concatenate(
                [(k_s[0:lo, :] * jnp.exp2(ref_b - b_s[0:lo, :])).astype(BF16),
                 jnp.zeros((CHUNK - lo, d), BF16)], axis=0)
            for h, cs in enumerate(heads):
                aoff_buf[par, h, lo:lo + SUB, :] = _dot_nt(q_t[:, cs], k_t[:, cs])

    def diag(c):
        par = c % 2
        b_s, q_s, k_s = b_buf.at[par], q_buf.at[par], k_buf.at[par]
        for i in range(n_sub):
            lo = i * SUB
            sub_rows = slice(lo, lo + SUB)
            for h0 in range(0, len(heads), HEAD_GROUP):
                group = heads[h0:h0 + HEAD_GROUP]
                acc = [[jnp.zeros((8, CHUNK), F32) for _ in range(SUB // 8)] for _ in group]
                for s in range(SUB):
                    sr = pl.ds(lo + s, 1)
                    for g in range(s // 8, SUB // 8):
                        ts = slice(lo + 8 * g, lo + 8 * g + 8)
                        pick = grp_lane == lo + s
                        if g == s // 8 and s % 8 > 0:
                            pick = pick & (grp_row >= s % 8)
                        for j, cs in enumerate(group):
                            p = (jnp.exp2(b_s[ts, cs] - b_s[sr, cs])
                                 * (q_s[ts, cs] * k_s[sr, cs]))
                            a = jnp.sum(p, axis=-1, keepdims=True)
                            acc[j][g] = jnp.where(pick, a, acc[j][g])
                for j in range(len(group)):
                    amat_buf[par, h0 + j, sub_rows, :] = (
                        aoff_buf[par, h0 + j, sub_rows, :] + jnp.concatenate(acc[j], axis=0)
                    ).astype(BF16)

    def tail(c):
        rs, par = rows(c), c % 2
        o_s = o_buf.at[c % 3]
        for h, cs in enumerate(heads):
            o = o_s[:, cs] + _dot(amat_buf[par, h], v_ref[rs, cs])
            ms = jnp.mean(o * o, axis=-1, keepdims=True)
            o = o * lax.rsqrt(ms + EPS) * ng_ref[:, cs] * g_ref[rs, cs].astype(F32)
            o_ref[rs, cs] = o.astype(o_ref.dtype)

    def steady(i, carry):
        tail(i - 2)
        front(i)
        diag(i - 1)
        return carry

    front(0)
    if n_chunks > 1:
        front(1)
        diag(0)
        lax.fori_loop(2, n_chunks, steady, 0)
        tail(n_chunks - 2)
    diag(n_chunks - 1)
    tail(n_chunks - 1)


def hg_chunk(q, k, v, lf, g, norm_g, lblk):
    bsz, s, d = q.shape
    n_heads = d // HG_D
    spec = pl.BlockSpec((None, lblk, d), lambda b, c: (b, c, 0))
    return pl.pallas_call(
        _hg_chunk_kernel,
        grid=(bsz, s // lblk),
        in_specs=[spec, spec, spec, spec, spec, pl.BlockSpec((1, d), lambda b, c: (0, 0))],
        out_specs=spec,
        out_shape=jax.ShapeDtypeStruct((bsz, s, d), BF16),
        scratch_shapes=[pltpu.VMEM((n_heads, HG_D, HG_D), F32),
                        pltpu.VMEM((2, CHUNK, d), F32),
                        pltpu.VMEM((2, CHUNK, d), F32),
                        pltpu.VMEM((2, CHUNK, d), F32),
                        pltpu.VMEM((3, CHUNK, d), F32),
                        pltpu.VMEM((2, n_heads, CHUNK, CHUNK), F32),
                        pltpu.VMEM((2, n_heads, CHUNK, CHUNK), BF16)],
        compiler_params=_cparams("parallel", "arbitrary"),
        name="hg_chunk",
    )(q, k, v, lf, g, norm_g.reshape(1, d))


def _ml_inproj_kernel(n_lin, tn, a_ref, w_ref, wg_ref, bg_ref, p_ref, gt_ref):
    a = a_ref[...]
    for j in range(w_ref.shape[1] // tn):
        cols = slice(j * tn, (j + 1) * tn)
        acc = _dot(a, w_ref[:, cols])
        p_ref[:, cols] = (acc if j < n_lin else _sigmoid(acc)).astype(BF16)
    gts = _dot(a, wg_ref[...]) + bg_ref[...]
    lane = lax.broadcasted_iota(jnp.int32, gts.shape, 1)
    log_sig = jnp.minimum(gts, 0.0) - jnp.log(1.0 + jnp.exp(-jnp.abs(gts)))
    gt_ref[...] = jnp.where(lane < ML_HEADS, gts, log_sig)


def ml_inproj(a, w_main, w_gate, b_gate, tm, tn, n_lin):
    t, d = a.shape
    n = w_main.shape[1]
    gl = w_gate.shape[1]
    return pl.pallas_call(
        functools.partial(_ml_inproj_kernel, n_lin, tn),
        grid=(t // tm,),
        in_specs=[pl.BlockSpec((tm, d), lambda i: (i, 0)),
                  _whole(w_main), _whole(w_gate), _whole(b_gate)],
        out_specs=[pl.BlockSpec((tm, n), lambda i: (i, 0)),
                   pl.BlockSpec((tm, gl), lambda i: (i, 0))],
        out_shape=[jax.ShapeDtypeStruct((t, n), BF16),
                   jax.ShapeDtypeStruct((t, gl), F32)],
        compiler_params=_cparams("parallel"),
        name="ml_inproj",
    )(a, w_main, w_gate, b_gate)


def _ml_chunk_kernel(qk_ref, v_ref, og_ref, gc_ref, gr_ref, ng_ref, o_ref,
                     c_ref, m_ref):
    lblk = v_ref.shape[0]
    nqk = ML_HEADS * ML_DQK

    @pl.when(pl.program_id(1) == 0)
    def _():
        c_ref[...] = jnp.zeros_like(c_ref)
        m_ref[...] = jnp.zeros_like(m_ref)

    row = lax.broadcasted_iota(jnp.int32, (CHUNK, CHUNK), 0)
    col = lax.broadcasted_iota(jnp.int32, (CHUNK, CHUNK), 1)
    causal = row >= col
    tril = causal.astype(F32)
    triu = (row <= col).astype(F32)
    ones_v = jnp.ones((CHUNK, ML_DV), BF16)
    scale = ML_DQK ** -0.5

    def chunk_body(c, carry):
        r0 = pl.multiple_of(c * CHUNK, CHUNK)
        rs = pl.ds(r0, CHUNK)
        g_cols = gc_ref[rs, :]
        g_rows = gr_ref[c]
        b_cols = _dot_f32(tril, g_cols)
        b_rows = _dot_f32(g_rows, triu)
        heads = range(ML_HEADS)
        qs = [qk_ref[rs, h * ML_DQK:(h + 1) * ML_DQK] for h in heads]
        ks = [qk_ref[rs, nqk + h * ML_DQK:nqk + (h + 1) * ML_DQK] for h in heads]
        v_exts = [jnp.concatenate([v_ref[rs, h * ML_DV:(h + 1) * ML_DV], ones_v], axis=1)
                  for h in heads]
        c_prevs = [c_ref[h] for h in heads]
        b_reps = [jnp.broadcast_to(b_cols[:, ML_HEADS + h:ML_HEADS + h + 1], (CHUNK, ML_DV))
                  for h in heads]
        li_reps = [jnp.broadcast_to(g_cols[:, h:h + 1], (CHUNK, ML_DV)) for h in heads]
        s_qk = [_dot_nt(qs[h], ks[h]) for h in heads]
        q_c = [_dot(qs[h], c_prevs[h].astype(BF16)) for h in heads]
        m_ts, d_ms, w_inters, kws, decs = [], [], [], [], []
        for h in heads:
            b_rep = b_reps[h]
            b_row = b_rows[ML_HEADS + h:ML_HEADS + h + 1, :]
            li_row = g_rows[h:h + 1, :]
            m_prev = m_ref[h:h + 1, :]
            log_d = jnp.where(causal, b_rep[:, :CHUNK] - b_row + li_row, NEG_BIG)
            inter = b_rep + m_prev
            m_t = jnp.maximum(inter, jnp.max(log_d, axis=-1, keepdims=True))
            m_ts.append(m_t)
            d_ms.append(jnp.exp(log_d - m_t[:, :CHUNK]))
            w_inters.append(jnp.exp(inter - m_t))
            b_last = b_rep[CHUNK - 1:CHUNK, :]
            log_w = b_last - b_rep + li_reps[h]
            m_new = jnp.maximum(b_last + m_prev, jnp.max(log_w, axis=0, keepdims=True))
            w = jnp.exp(log_w - m_new)
            decs.append(jnp.exp(b_last + m_prev - m_new))
            kws.append((ks[h].astype(F32) * (w[:, :ML_DQK] * scale)).astype(BF16))
            m_ref[h:h + 1, :] = m_new
        nums = []
        for h in heads:
            s_mat = (s_qk[h] * scale * d_ms[h]).astype(BF16)
            w2 = jnp.concatenate([w_inters[h], w_inters[h]], axis=1)
            nums.append(_dot(s_mat, v_exts[h]) + w2 * q_c[h])
            dec2 = jnp.concatenate([decs[h], decs[h]], axis=1)
            c_ref[h] = dec2 * c_prevs[h] + _dot_tn(kws[h], v_exts[h])
        for h in heads:
            den = nums[h][:, ML_DV:]
            hh = nums[h][:, :ML_DV] / jnp.maximum(jnp.abs(den), jnp.exp(-m_ts[h]))
            vs = pl.ds(h * ML_DV, ML_DV)
            ms = jnp.mean(hh * hh, axis=-1, keepdims=True)
            out = hh * lax.rsqrt(ms + EPS) * ng_ref[:, vs] * og_ref[rs, vs].astype(F32)
            o_ref[rs, vs] = out.astype(o_ref.dtype)
        return carry

    lax.fori_loop(0, lblk // CHUNK, chunk_body, 0)


def ml_chunk(proj, gates_cols, gates_rows, norm_g, lblk):
    bsz, s, _ = proj.shape
    d = ML_HEADS * ML_DV
    nqk2 = 2 * ML_HEADS * ML_DQK
    gl = gates_cols.shape[-1]
    return pl.pallas_call(
        _ml_chunk_kernel,
        grid=(bsz, s // lblk),
        in_specs=[pl.BlockSpec((None, lblk, nqk2), lambda b, c: (b, c, 0)),
                  pl.BlockSpec((None, lblk, d), lambda b, c: (b, c, nqk2 // d)),
                  pl.BlockSpec((None, lblk, d), lambda b, c: (b, c, nqk2 // d + 1)),
                  pl.BlockSpec((None, lblk, gl), lambda b, c: (b, c, 0)),
                  pl.BlockSpec((None, lblk // CHUNK, 2 * ML_HEADS, CHUNK),
                               lambda b, c: (b, c, 0, 0)),
                  pl.BlockSpec((1, d), lambda b, c: (0, 0))],
        out_specs=pl.BlockSpec((None, lblk, d), lambda b, c: (b, c, 0)),
        out_shape=jax.ShapeDtypeStruct((bsz, s, d), BF16),
        scratch_shapes=[pltpu.VMEM((ML_HEADS, ML_DQK, 2 * ML_DV), F32),
                        pltpu.VMEM((ML_HEADS, 128), F32)],
        compiler_params=_cparams("parallel", "arbitrary"),
        name="ml_chunk",
    )(proj, proj, proj, gates_cols, gates_rows, norm_g.reshape(1, d))


def _mem_kv_kernel(m_ref, g_ref, w_ref, kv_ref):
    mn = _rms_rows(m_ref[...], g_ref[...]).astype(BF16)
    kv_ref[...] = _dot(mn, w_ref[...]).astype(BF16)


def mem_kv(mem2d, g, wkv, tn):
    t, d = mem2d.shape
    n = wkv.shape[1]
    return pl.pallas_call(
        _mem_kv_kernel,
        grid=(n // tn,),
        in_specs=[pl.BlockSpec((t, d), lambda j: (0, 0)),
                  pl.BlockSpec((1, d), lambda j: (0, 0)),
                  pl.BlockSpec((d, tn), lambda j: (0, j))],
        out_specs=pl.BlockSpec((t, tn), lambda j: (0, j)),
        out_shape=jax.ShapeDtypeStruct((t, n), BF16),
        compiler_params=_cparams("parallel"),
        name="mem_kv",
    )(mem2d, g.reshape(1, d), wkv)


def _xattn_kernel(om_ref, wm_ref, gx_ref, k_ref, v_ref, wq_ref, wo_ref, h_ref, g_ref,
                  hn_ref, an_ref):
    d = om_ref.shape[1]
    hd = d // XA_HEADS
    scale = hd ** -0.5
    h_mix = h_ref[...] + _dot(om_ref[...], wm_ref[...])
    a = _rms_rows(h_mix, gx_ref[...]).astype(BF16)
    q = _dot(a, wq_ref[...]).astype(BF16)
    outs = []
    for hh in range(XA_HEADS):
        cs = slice(hh * hd, (hh + 1) * hd)
        s = _dot_nt(q[:, cs], k_ref[:, cs]) * scale
        p = jnp.exp(s - jnp.max(s, axis=-1, keepdims=True))
        p = p / jnp.sum(p, axis=-1, keepdims=True)
        outs.append(_dot(p.astype(BF16), v_ref[:, cs]).astype(BF16))
    o = jnp.concatenate(outs, axis=1)
    hn = h_mix + _dot(o, wo_ref[...])
    hn_ref[...] = hn
    an_ref[...] = _rms_rows(hn, g_ref[...]).astype(an_ref.dtype)


def mix_xattn_res_norm(o_mix, w_mix, g_xa, kv, wq, wo, h, g_next, tm):
    bsz, s, d = o_mix.shape
    n_mem = kv.shape[1]
    tok = pl.BlockSpec((None, tm, d), lambda b, i: (b, i, 0))
    g_xa, g_next = g_xa.reshape(1, d), g_next.reshape(1, d)
    return pl.pallas_call(
        _xattn_kernel,
        grid=(bsz, s // tm),
        in_specs=[tok, _whole(w_mix), _whole(g_xa),
                  pl.BlockSpec((None, n_mem, d), lambda b, i: (b, 0, 0)),
                  pl.BlockSpec((None, n_mem, d), lambda b, i: (b, 0, 1)),
                  _whole(wq), _whole(wo), tok, _whole(g_next)],
        out_specs=[tok, tok],
        out_shape=[jax.ShapeDtypeStruct((bsz, s, d), F32),
                   jax.ShapeDtypeStruct((bsz, s, d), BF16)],
        compiler_params=_cparams("parallel", "parallel"),
        name="mix_xattn_res_norm",
    )(o_mix, w_mix, g_xa, kv, kv, wq, wo, h, g_next)


def _ffn_kernel(tf, a_ref, ah_ref, wup_ref, cw_ref, cb_ref, wd_ref, h_ref, g_ref,
                hn_ref, an_ref, u_s, y_s):
    i = pl.program_id(1)
    tm = a_ref.shape[0]
    dff = wd_ref.shape[0]
    n_lane = tf // LANES

    halo = jnp.where(i > 0, ah_ref[...], jnp.zeros_like(ah_ref))
    a_ext = jnp.concatenate([halo, a_ref[...]], axis=0)

    for j in range(dff // tf):
        buf = j % 2
        for part in range(2):
            col = part * dff + j * tf
            u = _dot(a_ext, wup_ref[:, col:col + tf])
            for c in range(n_lane):
                u_s[buf, part, c] = u[:, c * LANES:(c + 1) * LANES]
        for c in range(n_lane):
            conv = []
            for part in range(2):
                lanes = slice(part * dff + j * tf + c * LANES,
                              part * dff + j * tf + (c + 1) * LANES)
                out = cb_ref[:, lanes]
                for tap in range(CONV_W):
                    sh = CONV_W - 1 - tap
                    out = out + (u_s[buf, part, c, HALO - sh:HALO - sh + tm, :]
                                 * cw_ref[tap:tap + 1, lanes])
                conv.append(out)
            gate, val = conv
            y_s[:, j * tf + c * LANES:j * tf + (c + 1) * LANES] = (
                gate * _sigmoid(gate) * val).astype(BF16)

    hn = h_ref[...] + _dot(y_s[...], wd_ref[...])
    hn_ref[...] = hn
    an_ref[...] = _rms_rows(hn, g_ref[...]).astype(an_ref.dtype)


def ffn_res_norm(a, w_up, conv_w, conv_b, w_down, h, g_next, a_dtype, tm, tf):
    bsz, s, d = a.shape
    dff = w_down.shape[0]
    hb = tm // HALO
    tok = pl.BlockSpec((None, tm, d), lambda b, i: (b, i, 0))
    whole = lambda arr: pl.BlockSpec(arr.shape, lambda b, i: (0, 0),
                                     pipeline_mode=pl.Buffered(1))
    conv_b = conv_b.reshape(1, -1)
    return pl.pallas_call(
        functools.partial(_ffn_kernel, tf),
        grid=(bsz, s // tm),
        in_specs=[tok,
                  pl.BlockSpec((None, HALO, d),
                               lambda b, i: (b, jnp.maximum(i * hb - 1, 0), 0)),
                  whole(w_up), whole(conv_w), whole(conv_b), whole(w_down),
                  tok,
                  pl.BlockSpec((1, d), lambda b, i: (0, 0))],
        out_specs=[tok, tok],
        out_shape=[jax.ShapeDtypeStruct((bsz, s, d), F32),
                   jax.ShapeDtypeStruct((bsz, s, d), a_dtype)],
        scratch_shapes=[pltpu.VMEM((2, 2, tf // LANES, HALO + tm, LANES), F32),
                        pltpu.VMEM((tm, dff), BF16)],
        compiler_params=_cparams("parallel", "parallel"),
        name="ffn_res_norm",
    )(a, a, w_up, conv_w, conv_b, w_down, h, g_next.reshape(1, d))


def _pick(n, pref):
    for c in pref:
        if n % c == 0:
            return c
    return n


def kernel(x, mem, norm_mix_g, norm_xa_g, norm_mem_g, norm_ffn_g, hg_w_in, hg_w_out, hg_norm_g, hg_lb_logits, ml_w_in, ml_b_gate, ml_w_out, ml_norm_g, xa_wq, xa_wkv, xa_wo, ffn_w_up, ffn_conv_w, ffn_conv_b, ffn_w_down, final_g):
    bsz, s, d = x.shape
    n_mem = mem.shape[1]
    depth = norm_mix_g.shape[0]
    t = bsz * s
    tm = _pick(s, (512, 256, 128, 64))
    lblk = _pick(s, (512, 256, 128, 64))
    tn = 256
    dff = ffn_w_down.shape[1]
    tf = _pick(dff, (256, 128))
    nqk2 = 2 * ML_HEADS * ML_DQK
    n_main = nqk2 + 2 * d
    gl = LANES

    h = x
    a = rmsnorm(x.reshape(t, d), norm_mix_g[0], BF16, tm)
    mem2d = mem.reshape(bsz * n_mem, d)
    r3 = lambda z: z.reshape(bsz, s, -1)

    for layer in range(depth):
        j = layer // 2
        if layer % 2 == 0:
            q, k, lf, v, g = hg_inproj(a, hg_w_in[j].astype(BF16), hg_lb_logits, layer, tm, tn)
            o = hg_chunk(r3(q), r3(k), r3(v), r3(lf), r3(g), hg_norm_g[j], lblk)
            w_out = hg_w_out[j]
        else:
            w_in = ml_w_in[j]
            w_gate = jnp.pad(w_in[:, n_main:], ((0, 0), (0, gl - 2 * ML_HEADS))).astype(BF16)
            b_gate = jnp.pad(ml_b_gate[j], (0, gl - 2 * ML_HEADS)).reshape(1, gl)
            proj, gts = ml_inproj(a, w_in[:, :n_main].astype(BF16), w_gate, b_gate,
                                  tm, tn, (nqk2 + d) // tn)
            gts = r3(gts)
            g_rows = gts[:, :, :2 * ML_HEADS].reshape(bsz, s // CHUNK, CHUNK, 2 * ML_HEADS)
            g_rows = g_rows.transpose(0, 1, 3, 2)
            o = ml_chunk(r3(proj), gts, g_rows, ml_norm_g[j], lblk)
            w_out = ml_w_out[j]

        kv = mem_kv(mem2d, norm_mem_g[layer], xa_wkv[layer].astype(BF16), 512)
        h, a = mix_xattn_res_norm(o, w_out.astype(BF16), norm_xa_g[layer],
                                  kv.reshape(bsz, n_mem, 2 * d),
                                  xa_wq[layer].astype(BF16), xa_wo[layer].astype(BF16),
                                  h, norm_ffn_g[layer], tm)

        last = layer == depth - 1
        g_next = final_g if last else norm_mix_g[layer + 1]
        h, a = ffn_res_norm(a, ffn_w_up[layer].astype(BF16), ffn_conv_w[layer],
                            ffn_conv_b[layer], ffn_w_down[layer].astype(BF16), h, g_next,
                            F32 if last else BF16, tm, tf)
        a = a.reshape(t, d) if not last else a

    return a.reshape(bsz, s, d)
```

```python
import functools

import jax
import jax.numpy as jnp
from jax import lax
from jax.experimental import pallas as pl
from jax.experimental.pallas import tpu as pltpu

F32 = jnp.float32
BF16 = jnp.bfloat16

EPS = 1e-6
NEG_BIG = -1e30
F_FLOOR = 1e-20
LOG2E = 1.4426950408889634
LANES = 128
CHUNK = 64
HG_D = 128
ML_HEADS = 8
ML_DQK = 64
ML_DV = 128
XA_HEADS = 4
CONV_W = 3
HALO = 16

VMEM_LIMIT = 56 * 1024 * 1024


def _cparams(*sem):
    return pltpu.CompilerParams(dimension_semantics=sem, vmem_limit_bytes=VMEM_LIMIT)


def _dot(a, b):
    return jnp.dot(a, b, preferred_element_type=F32)


def _dot_nt(a, b):
    return lax.dot_general(a, b, (((1,), (1,)), ((), ())), preferred_element_type=F32)


def _dot_tn(a, b):
    return lax.dot_general(a, b, (((0,), (0,)), ((), ())), preferred_element_type=F32)


def _split3(x):
    hi = x.astype(BF16)
    rest = x - hi.astype(F32)
    mid = rest.astype(BF16)
    lo = (rest - mid.astype(F32)).astype(BF16)
    return hi, mid, lo


def _tri_dot(tri, x):
    hi, mid, lo = _split3(x)
    return (_dot(tri, lo) + _dot(tri, mid)) + _dot(tri, hi)


def _dot_tri(x, tri):
    hi, mid, lo = _split3(x)
    return (_dot(lo, tri) + _dot(mid, tri)) + _dot(hi, tri)


def _sigmoid(x):
    return 0.5 * jnp.tanh(0.5 * x) + 0.5


def _silu(x):
    return x * _sigmoid(x)


def _sigmoid_pair(x):
    e = jnp.exp(-jnp.abs(x))
    big = 1.0 / (1.0 + e)
    small = e * big
    pos = x >= 0.0
    return jnp.where(pos, big, small), jnp.where(pos, small, big)


def _whole(arr):
    nd = arr.ndim
    return pl.BlockSpec(arr.shape, lambda *_: (0,) * nd, pipeline_mode=pl.Buffered(1))


def _rms_rows(x, g):
    ms = jnp.mean(x * x, axis=-1, keepdims=True)
    return x * lax.rsqrt(ms + EPS) * g


def _hg_inproj_kernel(layer, tn, a_ref, gin_ref, w_ref, lbl_ref,
                      q_ref, k_ref, lf_ref, v_ref, g_ref):
    a = a_ref[...]
    if a.dtype != BF16:
        a = _rms_rows(a, gin_ref[...]).astype(BF16)
    d = a.shape[1]
    depth = lbl_ref.shape[0]
    for j in range(d // tn):
        cols = slice(j * tn, (j + 1) * tn)
        proj = lambda grp: _dot(a, w_ref[:, grp * d + j * tn:grp * d + (j + 1) * tn])
        q_ref[:, cols] = _silu(proj(0)).astype(BF16)
        v_ref[:, cols] = proj(2).astype(BF16)
        g_ref[:, cols] = _silu(proj(3)).astype(BF16)
        rows = [lbl_ref[l:l + 1, cols] for l in range(depth)]
        mx = functools.reduce(jnp.maximum, rows)
        es = [jnp.exp(r - mx) for r in rows]
        tot = functools.reduce(lambda x, y: x + y, es)
        ps = [e / tot for e in es]
        lb = functools.reduce(lambda x, y: x + y, ps[:layer + 1]) - ps[0]
        sig_pos, sig_neg = _sigmoid_pair(proj(1))
        f = lb + (1.0 - lb) * sig_pos
        lf_ref[:, cols] = jnp.log(jnp.maximum(f, F_FLOOR))
        k_ref[:, cols] = ((1.0 - lb) * sig_neg).astype(BF16)


def hg_inproj(a, g_in, w_in, lb_logits, layer, tm, tn):
    t, d = a.shape
    tok = pl.BlockSpec((tm, d), lambda i: (i, 0))
    g_in = g_in.reshape(1, d)
    return pl.pallas_call(
        functools.partial(_hg_inproj_kernel, layer, tn),
        grid=(t // tm,),
        in_specs=[tok, _whole(g_in), _whole(w_in), _whole(lb_logits)],
        out_specs=[tok] * 5,
        out_shape=[jax.ShapeDtypeStruct((t, d), BF16),
                   jax.ShapeDtypeStruct((t, d), BF16),
                   jax.ShapeDtypeStruct((t, d), F32),
                   jax.ShapeDtypeStruct((t, d), BF16),
                   jax.ShapeDtypeStruct((t, d), BF16)],
        compiler_params=_cparams("parallel"),
        name="hg_inproj",
    )(a, g_in, w_in, lb_logits)


def _hg_chunk_kernel(q_ref, k_ref, v_ref, lf_ref, g_ref, ng_ref, o_ref,
                     st_ref, b_buf, q_buf, k_buf, o_buf, amat_buf):
    lblk, d = q_ref.shape
    n_chunks = lblk // CHUNK
    n_grp = CHUNK // 8
    heads = [slice(h * HG_D, (h + 1) * HG_D) for h in range(d // HG_D)]

    @pl.when(pl.program_id(1) == 0)
    def _():
        st_ref[...] = jnp.zeros_like(st_ref)

    t_idx = lax.broadcasted_iota(jnp.int32, (CHUNK, CHUNK), 0)
    s_idx = lax.broadcasted_iota(jnp.int32, (CHUNK, CHUNK), 1)
    tril = (t_idx >= s_idx).astype(BF16)
    pair_code = jnp.where(t_idx >= s_idx, t_idx ^ s_idx, -1)
    grp_row = lax.broadcasted_iota(jnp.int32, (8, HG_D), 0)
    zeros_grp = jnp.zeros((8, HG_D), F32)

    def rows(c):
        return pl.ds(pl.multiple_of(c * CHUNK, CHUNK), CHUNK)

    def split_ref(b_s, g, m, cs):
        if 2 * m >= 8:
            r = (8 * g) // (2 * m) * (2 * m) + m - 1
            return b_s[pl.ds(r, 1), cs]
        tile = b_s[pl.ds(8 * g + m - 1, 1), cs]
        for blk in range(1, 8 // (2 * m)):
            tile = jnp.where(grp_row >= blk * 2 * m,
                             b_s[pl.ds(8 * g + blk * 2 * m + m - 1, 1), cs], tile)
        return tile

    def prep(c):
        rs, par = rows(c), c % 2
        b_buf[par] = _tri_dot(tril, lf_ref[rs, :]) * LOG2E
        q_buf[par] = q_ref[rs, :].astype(F32)
        k_buf[par] = k_ref[rs, :].astype(F32)

    def front(c):
        rs, par = rows(c), c % 2
        b_s, q_s, k_s, o_s = b_buf.at[par], q_buf.at[par], k_buf.at[par], o_buf.at[par]
        vb = v_ref[rs, :]
        b = b_s[...]
        b_last = b_s[CHUNK - 1:CHUNK, :]
        qb = (q_s[...] * jnp.exp2(b)).astype(BF16)
        k_dec = (k_s[...] * jnp.exp2(b_last - b)).astype(BF16)
        dec = jnp.exp2(b_last)
        for h, cs in enumerate(heads):
            st = st_ref[h]
            o_s[:, cs] = _dot_nt(qb[:, cs], st.astype(BF16))
            st_ref[h] = st * dec[:, cs] + _dot_tn(vb[:, cs], k_dec[:, cs])

        acc = [jnp.where(pair_code == 0, _dot_nt(q_ref[rs, cs], k_ref[rs, cs]), 0.0)
               for cs in heads]
        m = 1
        while m < CHUNK:
            take = pair_code >= m
            for h, cs in enumerate(heads):
                q_parts, k_parts = [], []
                for g in range(n_grp):
                    ts = slice(8 * g, 8 * g + 8)
                    whole = 2 * m >= 16
                    upper = (8 * g) % (2 * m) >= m
                    b_g, b_r = b_s[ts, cs], split_ref(b_s, g, m, cs)
                    q_parts.append(q_s[ts, cs] * jnp.exp2(b_g - b_r)
                                   if not whole or upper else zeros_grp)
                    k_parts.append(k_s[ts, cs] * jnp.exp2(b_r - b_g)
                                   if not whole or not upper else zeros_grp)
                res = _dot_nt(jnp.concatenate(q_parts, axis=0).astype(BF16),
                              jnp.concatenate(k_parts, axis=0).astype(BF16))
                acc[h] = jnp.where(take, res, acc[h])
            m *= 2
        for h in range(len(heads)):
            amat_buf[par, h] = acc[h].astype(BF16)

    def tail(c):
        rs, par = rows(c), c % 2
        o_s = o_buf.at[par]
        for h, cs in enumerate(heads):
            o = o_s[:, cs] + _dot(amat_buf[par, h], v_ref[rs, cs])
            ms = jnp.mean(o * o, axis=-1, keepdims=True)
            o = o * lax.rsqrt(ms + EPS) * ng_ref[:, cs] * g_ref[rs, cs].astype(F32)
            o_ref[rs, cs] = o.astype(o_ref.dtype)

    def steady(i, carry):
        tail(i - 1)
        prep(i + 1)
        front(i)
        return carry

    prep(0)
    if n_chunks > 1:
        prep(1)
        front(0)
        lax.fori_loop(1, n_chunks - 1, steady, 0)
        tail(n_chunks - 2)
    front(n_chunks - 1)
    tail(n_chunks - 1)


def hg_chunk(q, k, v, lf, g, norm_g, lblk):
    bsz, s, d = q.shape
    n_heads = d // HG_D
    spec = pl.BlockSpec((None, lblk, d), lambda b, c: (b, c, 0))
    return pl.pallas_call(
        _hg_chunk_kernel,
        grid=(bsz, s // lblk),
        in_specs=[spec, spec, spec, spec, spec, pl.BlockSpec((1, d), lambda b, c: (0, 0))],
        out_specs=spec,
        out_shape=jax.ShapeDtypeStruct((bsz, s, d), BF16),
        scratch_shapes=[pltpu.VMEM((n_heads, HG_D, HG_D), F32),
                        pltpu.VMEM((2, CHUNK, d), F32),
                        pltpu.VMEM((2, CHUNK, d), F32),
                        pltpu.VMEM((2, CHUNK, d), F32),
                        pltpu.VMEM((2, CHUNK, d), F32),
                        pltpu.VMEM((2, n_heads, CHUNK, CHUNK), BF16)],
        compiler_params=_cparams("parallel", "arbitrary"),
        name="hg_chunk",
    )(q, k, v, lf, g, norm_g.reshape(1, d))


def _ml_inproj_kernel(n_lin, tn, a_ref, w_ref, wg_ref, bg_ref, p_ref, gt_ref):
    a = a_ref[...]
    for j in range(w_ref.shape[1] // tn):
        cols = slice(j * tn, (j + 1) * tn)
        acc = _dot(a, w_ref[:, cols])
        p_ref[:, cols] = (acc if j < n_lin else _sigmoid(acc)).astype(BF16)
    gts = _dot(a, wg_ref[...]) + bg_ref[...]
    lane = lax.broadcasted_iota(jnp.int32, gts.shape, 1)
    log_sig = jnp.minimum(gts, 0.0) - jnp.log(1.0 + jnp.exp(-jnp.abs(gts)))
    gt_ref[...] = jnp.where(lane < ML_HEADS, gts, log_sig)


def ml_inproj(a, w_main, w_gate, b_gate, tm, tn, n_lin):
    t, d = a.shape
    n = w_main.shape[1]
    gl = w_gate.shape[1]
    return pl.pallas_call(
        functools.partial(_ml_inproj_kernel, n_lin, tn),
        grid=(t // tm,),
        in_specs=[pl.BlockSpec((tm, d), lambda i: (i, 0)),
                  _whole(w_main), _whole(w_gate), _whole(b_gate)],
        out_specs=[pl.BlockSpec((tm, n), lambda i: (i, 0)),
                   pl.BlockSpec((tm, gl), lambda i: (i, 0))],
        out_shape=[jax.ShapeDtypeStruct((t, n), BF16),
                   jax.ShapeDtypeStruct((t, gl), F32)],
        compiler_params=_cparams("parallel"),
        name="ml_inproj",
    )(a, w_main, w_gate, b_gate)


def _ml_chunk_kernel(qk_ref, v_ref, og_ref, gc_ref, gr_ref, ng_ref, o_ref,
                     c_ref, m_ref):
    lblk = v_ref.shape[0]
    nqk = ML_HEADS * ML_DQK

    @pl.when(pl.program_id(1) == 0)
    def _():
        c_ref[...] = jnp.zeros_like(c_ref)
        m_ref[...] = jnp.zeros_like(m_ref)

    row = lax.broadcasted_iota(jnp.int32, (CHUNK, CHUNK), 0)
    col = lax.broadcasted_iota(jnp.int32, (CHUNK, CHUNK), 1)
    causal = row >= col
    tril = causal.astype(BF16)
    triu = (row <= col).astype(BF16)
    ones_v = jnp.ones((CHUNK, ML_DV), BF16)
    scale = ML_DQK ** -0.5

    def chunk_body(c, carry):
        r0 = pl.multiple_of(c * CHUNK, CHUNK)
        rs = pl.ds(r0, CHUNK)
        g_cols = gc_ref[rs, :]
        g_rows = gr_ref[c]
        b_cols = _tri_dot(tril, g_cols)
        b_rows = _dot_tri(g_rows, triu)
        heads = range(ML_HEADS)
        qs = [qk_ref[rs, h * ML_DQK:(h + 1) * ML_DQK] for h in heads]
        ks = [qk_ref[rs, nqk + h * ML_DQK:nqk + (h + 1) * ML_DQK] for h in heads]
        v_exts = [jnp.concatenate([v_ref[rs, h * ML_DV:(h + 1) * ML_DV], ones_v], axis=1)
                  for h in heads]
        c_prevs = [c_ref[h] for h in heads]
        b_reps = [jnp.broadcast_to(b_cols[:, ML_HEADS + h:ML_HEADS + h + 1], (CHUNK, ML_DV))
                  for h in heads]
        li_reps = [jnp.broadcast_to(g_cols[:, h:h + 1], (CHUNK, ML_DV)) for h in heads]
        s_qk = [_dot_nt(qs[h], ks[h]) for h in heads]
        q_c = [_dot(qs[h], c_prevs[h].astype(BF16)) for h in heads]
        m_ts, d_ms, w_inters, kws, decs = [], [], [], [], []
        for h in heads:
            b_rep = b_reps[h]
            b_row = b_rows[ML_HEADS + h:ML_HEADS + h + 1, :]
            li_row = g_rows[h:h + 1, :]
            m_prev = m_ref[h:h + 1, :]
            log_d = jnp.where(causal, b_rep[:, :CHUNK] - b_row + li_row, NEG_BIG)
            inter = b_rep + m_prev
            m_t = jnp.maximum(inter, jnp.max(log_d, axis=-1, keepdims=True))
            m_ts.append(m_t)
            d_ms.append(jnp.exp(log_d - m_t[:, :CHUNK]))
            w_inters.append(jnp.exp(inter - m_t))
            b_last = b_rep[CHUNK - 1:CHUNK, :]
            log_w = b_last - b_rep + li_reps[h]
            m_new = jnp.maximum(b_last + m_prev, jnp.max(log_w, axis=0, keepdims=True))
            w = jnp.exp(log_w - m_new)
            decs.append(jnp.exp(b_last + m_prev - m_new))
            kws.append((ks[h].astype(F32) * (w[:, :ML_DQK] * scale)).astype(BF16))
            m_ref[h:h + 1, :] = m_new
        nums = []
        for h in heads:
            s_mat = (s_qk[h] * scale * d_ms[h]).astype(BF16)
            w2 = jnp.concatenate([w_inters[h], w_inters[h]], axis=1)
            nums.append(_dot(s_mat, v_exts[h]) + w2 * q_c[h])
            dec2 = jnp.concatenate([decs[h], decs[h]], axis=1)
            c_ref[h] = dec2 * c_prevs[h] + _dot_tn(kws[h], v_exts[h])
        for h in heads:
            den = nums[h][:, ML_DV:]
            hh = nums[h][:, :ML_DV] / jnp.maximum(jnp.abs(den), jnp.exp(-m_ts[h]))
            vs = pl.ds(h * ML_DV, ML_DV)
            ms = jnp.mean(hh * hh, axis=-1, keepdims=True)
            out = hh * lax.rsqrt(ms + EPS) * ng_ref[:, vs] * og_ref[rs, vs].astype(F32)
            o_ref[rs, vs] = out.astype(o_ref.dtype)
        return carry

    lax.fori_loop(0, lblk // CHUNK, chunk_body, 0)


def ml_chunk(proj, gates_cols, gates_rows, norm_g, lblk):
    bsz, s, _ = proj.shape
    d = ML_HEADS * ML_DV
    nqk2 = 2 * ML_HEADS * ML_DQK
    gl = gates_cols.shape[-1]
    return pl.pallas_call(
        _ml_chunk_kernel,
        grid=(bsz, s // lblk),
        in_specs=[pl.BlockSpec((None, lblk, nqk2), lambda b, c: (b, c, 0)),
                  pl.BlockSpec((None, lblk, d), lambda b, c: (b, c, nqk2 // d)),
                  pl.BlockSpec((None, lblk, d), lambda b, c: (b, c, nqk2 // d + 1)),
                  pl.BlockSpec((None, lblk, gl), lambda b, c: (b, c, 0)),
                  pl.BlockSpec((None, lblk // CHUNK, 2 * ML_HEADS, CHUNK),
                               lambda b, c: (b, c, 0, 0)),
                  pl.BlockSpec((1, d), lambda b, c: (0, 0))],
        out_specs=pl.BlockSpec((None, lblk, d), lambda b, c: (b, c, 0)),
        out_shape=jax.ShapeDtypeStruct((bsz, s, d), BF16),
        scratch_shapes=[pltpu.VMEM((ML_HEADS, ML_DQK, 2 * ML_DV), F32),
                        pltpu.VMEM((ML_HEADS, 128), F32)],
        compiler_params=_cparams("parallel", "arbitrary"),
        name="ml_chunk",
    )(proj, proj, proj, gates_cols, gates_rows, norm_g.reshape(1, d))


def _mem_kv_kernel(m_ref, g_ref, w_ref, kv_ref):
    mn = _rms_rows(m_ref[...], g_ref[...]).astype(BF16)
    kv_ref[...] = _dot(mn, w_ref[...]).astype(BF16)


def mem_kv(mem2d, g, wkv, tn):
    t, d = mem2d.shape
    n = wkv.shape[1]
    return pl.pallas_call(
        _mem_kv_kernel,
        grid=(n // tn,),
        in_specs=[pl.BlockSpec((t, d), lambda j: (0, 0)),
                  pl.BlockSpec((1, d), lambda j: (0, 0)),
                  pl.BlockSpec((d, tn), lambda j: (0, j))],
        out_specs=pl.BlockSpec((t, tn), lambda j: (0, j)),
        out_shape=jax.ShapeDtypeStruct((t, n), BF16),
        compiler_params=_cparams("parallel"),
        name="mem_kv",
    )(mem2d, g.reshape(1, d), wkv)


def _xattn_kernel(om_ref, wm_ref, gx_ref, k_ref, v_ref, wq_ref, wo_ref, h_ref, g_ref,
                  hn_ref, an_ref):
    d = om_ref.shape[1]
    hd = d // XA_HEADS
    scale = hd ** -0.5
    h_mix = h_ref[...] + _dot(om_ref[...], wm_ref[...])
    a = _rms_rows(h_mix, gx_ref[...]).astype(BF16)
    q = _dot(a, wq_ref[...]).astype(BF16)
    outs = []
    for hh in range(XA_HEADS):
        cs = slice(hh * hd, (hh + 1) * hd)
        s = _dot_nt(q[:, cs], k_ref[:, cs]) * scale
        p = jnp.exp(s - jnp.max(s, axis=-1, keepdims=True))
        p = p / jnp.sum(p, axis=-1, keepdims=True)
        outs.append(_dot(p.astype(BF16), v_ref[:, cs]).astype(BF16))
    o = jnp.concatenate(outs, axis=1)
    hn = h_mix + _dot(o, wo_ref[...])
    hn_ref[...] = hn
    an_ref[...] = _rms_rows(hn, g_ref[...]).astype(an_ref.dtype)


def mix_xattn_res_norm(o_mix, w_mix, g_xa, kv, wq, wo, h, g_next, tm):
    bsz, s, d = o_mix.shape
    n_mem = kv.shape[1]
    tok = pl.BlockSpec((None, tm, d), lambda b, i: (b, i, 0))
    g_xa, g_next = g_xa.reshape(1, d), g_next.reshape(1, d)
    return pl.pallas_call(
        _xattn_kernel,
        grid=(bsz, s // tm),
        in_specs=[tok, _whole(w_mix), _whole(g_xa),
                  pl.BlockSpec((None, n_mem, d), lambda b, i: (b, 0, 0)),
                  pl.BlockSpec((None, n_mem, d), lambda b, i: (b, 0, 1)),
                  _whole(wq), _whole(wo), tok, _whole(g_next)],
        out_specs=[tok, tok],
        out_shape=[jax.ShapeDtypeStruct((bsz, s, d), F32),
                   jax.ShapeDtypeStruct((bsz, s, d), BF16)],
        compiler_params=_cparams("parallel", "parallel"),
        name="mix_xattn_res_norm",
    )(o_mix, w_mix, g_xa, kv, kv, wq, wo, h, g_next)


def _ffn_kernel(tf, a_ref, ah_ref, wup_ref, cw_ref, cb_ref, wd_ref, h_ref, g_ref,
                hn_ref, an_ref, u_s, y_s):
    i = pl.program_id(1)
    tm = a_ref.shape[0]
    dff = wd_ref.shape[0]
    n_lane = tf // LANES

    halo = jnp.where(i > 0, ah_ref[...], jnp.zeros_like(ah_ref))
    a_ext = jnp.concatenate([halo, a_ref[...]], axis=0)

    for j in range(dff // tf):
        buf = j % 2
        for part in range(2):
            col = part * dff + j * tf
            u = _dot(a_ext, wup_ref[:, col:col + tf])
            for c in range(n_lane):
                u_s[buf, part, c] = u[:, c * LANES:(c + 1) * LANES]
        for c in range(n_lane):
            conv = []
            for part in range(2):
                lanes = slice(part * dff + j * tf + c * LANES,
                              part * dff + j * tf + (c + 1) * LANES)
                out = cb_ref[:, lanes]
                for tap in range(CONV_W):
                    sh = CONV_W - 1 - tap
                    out = out + (u_s[buf, part, c, HALO - sh:HALO - sh + tm, :]
                                 * cw_ref[tap:tap + 1, lanes])
                conv.append(out)
            gate, val = conv
            y_s[:, j * tf + c * LANES:j * tf + (c + 1) * LANES] = (
                gate * _sigmoid(gate) * val).astype(BF16)

    hn = h_ref[...] + _dot(y_s[...], wd_ref[...])
    hn_ref[...] = hn
    an_ref[...] = _rms_rows(hn, g_ref[...]).astype(an_ref.dtype)


def ffn_res_norm(a, w_up, conv_w, conv_b, w_down, h, g_next, a_dtype, tm, tf):
    bsz, s, d = a.shape
    dff = w_down.shape[0]
    hb = tm // HALO
    tok = pl.BlockSpec((None, tm, d), lambda b, i: (b, i, 0))
    conv_b = conv_b.reshape(1, -1)
    g_next = g_next.reshape(1, d)
    return pl.pallas_call(
        functools.partial(_ffn_kernel, tf),
        grid=(bsz, s // tm),
        in_specs=[tok,
                  pl.BlockSpec((None, HALO, d),
                               lambda b, i: (b, jnp.maximum(i * hb - 1, 0), 0)),
                  _whole(w_up), _whole(conv_w), _whole(conv_b), _whole(w_down),
                  tok, _whole(g_next)],
        out_specs=[tok, tok],
        out_shape=[jax.ShapeDtypeStruct((bsz, s, d), F32),
                   jax.ShapeDtypeStruct((bsz, s, d), a_dtype)],
        scratch_shapes=[pltpu.VMEM((2, 2, tf // LANES, HALO + tm, LANES), F32),
                        pltpu.VMEM((tm, dff), BF16)],
        compiler_params=_cparams("parallel", "parallel"),
        name="ffn_res_norm",
    )(a, a, w_up, conv_w, conv_b, w_down, h, g_next)


def _pick(n, pref):
    for c in pref:
        if n % c == 0:
            return c
    return n


def kernel(x, mem, norm_mix_g, norm_xa_g, norm_mem_g, norm_ffn_g, hg_w_in, hg_w_out, hg_norm_g, hg_lb_logits, ml_w_in, ml_b_gate, ml_w_out, ml_norm_g, xa_wq, xa_wkv, xa_wo, ffn_w_up, ffn_conv_w, ffn_conv_b, ffn_w_down, final_g):
    bsz, s, d = x.shape
    n_mem = mem.shape[1]
    depth = norm_mix_g.shape[0]
    t = bsz * s
    tm = _pick(s, (512, 256, 128, 64))
    lblk = _pick(s, (512, 256, 128, 64))
    tn = 256
    dff = ffn_w_down.shape[1]
    tf = _pick(dff, (256, 128))
    nqk2 = 2 * ML_HEADS * ML_DQK
    n_main = nqk2 + 2 * d
    gl = LANES

    h = x
    a = x.reshape(t, d)
    mem2d = mem.reshape(bsz * n_mem, d)
    r3 = lambda z: z.reshape(bsz, s, -1)

    for layer in range(depth):
        j = layer // 2
        if layer % 2 == 0:
            q, k, lf, v, g = hg_inproj(a, norm_mix_g[layer], hg_w_in[j].astype(BF16),
                                       hg_lb_logits, layer, tm, tn)
            o = hg_chunk(r3(q), r3(k), r3(v), r3(lf), r3(g), hg_norm_g[j], lblk)
            w_out = hg_w_out[j]
        else:
            w_in = ml_w_in[j]
            w_gate = jnp.pad(w_in[:, n_main:], ((0, 0), (0, gl - 2 * ML_HEADS))).astype(BF16)
            b_gate = jnp.pad(ml_b_gate[j], (0, gl - 2 * ML_HEADS)).reshape(1, gl)
            proj, gts = ml_inproj(a, w_in[:, :n_main].astype(BF16), w_gate, b_gate,
                                  tm, tn, (nqk2 + d) // tn)
            gts = r3(gts)
            g_rows = gts[:, :, :2 * ML_HEADS].reshape(bsz, s // CHUNK, CHUNK, 2 * ML_HEADS)
            g_rows = g_rows.transpose(0, 1, 3, 2)
            o = ml_chunk(r3(proj), gts, g_rows, ml_norm_g[j], lblk)
            w_out = ml_w_out[j]

        kv = mem_kv(mem2d, norm_mem_g[layer], xa_wkv[layer].astype(BF16), 512)
        h, a = mix_xattn_res_norm(o, w_out.astype(BF16), norm_xa_g[layer],
                                  kv.reshape(bsz, n_mem, 2 * d),
                                  xa_wq[layer].astype(BF16), xa_wo[layer].astype(BF16),
                                  h, norm_ffn_g[layer], tm)

        last = layer == depth - 1
        g_next = final_g if last else norm_mix_g[layer + 1]
        h, a = ffn_res_norm(a, ffn_w_up[layer].astype(BF16), ffn_conv_w[layer],
                            ffn_conv_b[layer], ffn_w_down[layer].astype(BF16), h, g_next,
                            F32 if last else BF16, tm, tf)
        a = a.reshape(t, d) if not last else a

    return a.reshape(bsz, s, d)
```

```python
import functools

import jax
import jax.numpy as jnp
from jax import lax
from jax.experimental import pallas as pl
from jax.experimental.pallas import tpu as pltpu

F32 = jnp.float32
BF16 = jnp.bfloat16

EPS = 1e-6
NEG_BIG = -1e30
F_FLOOR = 1e-20
LOG2E = 1.4426950408889634
LANES = 128
CHUNK = 64
HG_D = 128
ML_HEADS = 8
ML_DQK = 64
ML_DV = 128
XA_HEADS = 4
CONV_W = 3
HALO = 16

VMEM_LIMIT = 56 * 1024 * 1024


def _cparams(*sem):
    return pltpu.CompilerParams(dimension_semantics=sem, vmem_limit_bytes=VMEM_LIMIT)


def _dot(a, b):
    return jnp.dot(a, b, preferred_element_type=F32)


def _dot_nt(a, b):
    return lax.dot_general(a, b, (((1,), (1,)), ((), ())), preferred_element_type=F32)


def _dot_tn(a, b):
    return lax.dot_general(a, b, (((0,), (0,)), ((), ())), preferred_element_type=F32)


def _split3(x):
    hi = x.astype(BF16)
    rest = x - hi.astype(F32)
    mid = rest.astype(BF16)
    lo = (rest - mid.astype(F32)).astype(BF16)
    return hi, mid, lo


def _tri_dot(tri, x):
    hi, mid, lo = _split3(x)
    return (_dot(tri, lo) + _dot(tri, mid)) + _dot(tri, hi)


def _dot_tri(x, tri):
    hi, mid, lo = _split3(x)
    return (_dot(lo, tri) + _dot(mid, tri)) + _dot(hi, tri)


def _sigmoid(x):
    return 0.5 * jnp.tanh(0.5 * x) + 0.5


def _silu(x):
    return x * _sigmoid(x)


def _sigmoid_pair(x):
    e = jnp.exp(-jnp.abs(x))
    big = 1.0 / (1.0 + e)
    small = e * big
    pos = x >= 0.0
    return jnp.where(pos, big, small), jnp.where(pos, small, big)


def _whole(arr):
    nd = arr.ndim
    return pl.BlockSpec(arr.shape, lambda *_: (0,) * nd, pipeline_mode=pl.Buffered(1))


def _pipeline3(n, prep, front, tail):
    prep(0, 0)
    if n == 1:
        front(0, 0)
        tail(0, 0)
        return
    assert n % 2 == 0
    prep(1, 1)
    front(0, 0)

    def two_steps(j, carry):
        for par in (1, 0):
            i = 2 * j + 2 - par
            tail(i - 1, 1 - par)
            prep(i + 1, 1 - par)
            front(i, par)
        return carry

    lax.fori_loop(0, (n - 2) // 2, two_steps, 0)
    tail(n - 2, 0)
    front(n - 1, 1)
    tail(n - 1, 1)


def _rms_rows(x, g):
    ms = jnp.mean(x * x, axis=-1, keepdims=True)
    return x * lax.rsqrt(ms + EPS) * g


def _hg_inproj_kernel(layer, tn, a_ref, gin_ref, w_ref, lbl_ref,
                      q_ref, k_ref, lf_ref, v_ref, g_ref):
    a = a_ref[...]
    if a.dtype != BF16:
        a = _rms_rows(a, gin_ref[...]).astype(BF16)
    d = a.shape[1]
    depth = lbl_ref.shape[0]
    for j in range(d // tn):
        cols = slice(j * tn, (j + 1) * tn)
        proj = lambda grp: _dot(a, w_ref[:, grp * d + j * tn:grp * d + (j + 1) * tn])
        q_ref[:, cols] = _silu(proj(0)).astype(BF16)
        v_ref[:, cols] = proj(2).astype(BF16)
        g_ref[:, cols] = _silu(proj(3)).astype(BF16)
        rows = [lbl_ref[l:l + 1, cols] for l in range(depth)]
        mx = functools.reduce(jnp.maximum, rows)
        es = [jnp.exp(r - mx) for r in rows]
        tot = functools.reduce(lambda x, y: x + y, es)
        ps = [e / tot for e in es]
        lb = functools.reduce(lambda x, y: x + y, ps[:layer + 1]) - ps[0]
        sig_pos, sig_neg = _sigmoid_pair(proj(1))
        f = lb + (1.0 - lb) * sig_pos
        lf_ref[:, cols] = jnp.log(jnp.maximum(f, F_FLOOR))
        k_ref[:, cols] = ((1.0 - lb) * sig_neg).astype(BF16)


def hg_inproj(a, g_in, w_in, lb_logits, layer, tm, tn):
    t, d = a.shape
    tok = pl.BlockSpec((tm, d), lambda i: (i, 0))
    g_in = g_in.reshape(1, d)
    return pl.pallas_call(
        functools.partial(_hg_inproj_kernel, layer, tn),
        grid=(t // tm,),
        in_specs=[tok, _whole(g_in), _whole(w_in), _whole(lb_logits)],
        out_specs=[tok] * 5,
        out_shape=[jax.ShapeDtypeStruct((t, d), BF16),
                   jax.ShapeDtypeStruct((t, d), BF16),
                   jax.ShapeDtypeStruct((t, d), F32),
                   jax.ShapeDtypeStruct((t, d), BF16),
                   jax.ShapeDtypeStruct((t, d), BF16)],
        compiler_params=_cparams("parallel"),
        name="hg_inproj",
    )(a, g_in, w_in, lb_logits)


def _hg_chunk_kernel(q_ref, k_ref, v_ref, lf_ref, g_ref, ng_ref, o_ref,
                     st_ref, b_buf, q_buf, k_buf, o_buf, amat_buf):
    lblk, d = q_ref.shape
    n_chunks = lblk // CHUNK
    n_grp = CHUNK // 8
    heads = [slice(h * HG_D, (h + 1) * HG_D) for h in range(d // HG_D)]

    @pl.when(pl.program_id(1) == 0)
    def _():
        st_ref[...] = jnp.zeros_like(st_ref)

    t_idx = lax.broadcasted_iota(jnp.int32, (CHUNK, CHUNK), 0)
    s_idx = lax.broadcasted_iota(jnp.int32, (CHUNK, CHUNK), 1)
    tril = (t_idx >= s_idx).astype(BF16)
    pair_code = jnp.where(t_idx >= s_idx, t_idx ^ s_idx, -1)
    grp_row = lax.broadcasted_iota(jnp.int32, (8, HG_D), 0)
    zeros_grp = jnp.zeros((8, HG_D), F32)
    splits = [1 << p for p in range((CHUNK - 1).bit_length())]

    def rows(c):
        return pl.ds(pl.multiple_of(c * CHUNK, CHUNK), CHUNK)

    def split_ref(b_s, g, m, cs):
        if 2 * m >= 8:
            r = (8 * g) // (2 * m) * (2 * m) + m - 1
            return b_s[pl.ds(r, 1), cs]
        tile = b_s[pl.ds(8 * g + m - 1, 1), cs]
        for blk in range(1, 8 // (2 * m)):
            tile = jnp.where(grp_row >= blk * 2 * m,
                             b_s[pl.ds(8 * g + blk * 2 * m + m - 1, 1), cs], tile)
        return tile

    def prep(c, par):
        rs = rows(c)
        b_buf[par] = _tri_dot(tril, lf_ref[rs, :]) * LOG2E
        q_buf[par] = q_ref[rs, :].astype(F32)
        k_buf[par] = k_ref[rs, :].astype(F32)

    def front(c, par):
        rs = rows(c)
        b_s, q_s, k_s, o_s = b_buf.at[par], q_buf.at[par], k_buf.at[par], o_buf.at[par]
        vb = v_ref[rs, :]
        b = b_s[...]
        b_last = b_s[CHUNK - 1:CHUNK, :]
        qb = (q_s[...] * jnp.exp2(b)).astype(BF16)
        k_dec = (k_s[...] * jnp.exp2(b_last - b)).astype(BF16)
        dec = jnp.exp2(b_last)
        for h, cs in enumerate(heads):
            st = st_ref[h]
            o_s[:, cs] = _dot_nt(qb[:, cs], st.astype(BF16))
            st_ref[h] = st * dec[:, cs] + _dot_tn(vb[:, cs], k_dec[:, cs])

        for h, cs in enumerate(heads):
            acc = jnp.where(pair_code == 0, _dot_nt(q_ref[rs, cs], k_ref[rs, cs]), 0.0)
            for m in splits:
                q_parts, k_parts = [], []
                for g in range(n_grp):
                    ts = slice(8 * g, 8 * g + 8)
                    whole = 2 * m >= 16
                    upper = (8 * g) % (2 * m) >= m
                    b_g, b_r = b_s[ts, cs], split_ref(b_s, g, m, cs)
                    q_parts.append(q_s[ts, cs] * jnp.exp2(b_g - b_r)
                                   if not whole or upper else zeros_grp)
                    k_parts.append(k_s[ts, cs] * jnp.exp2(b_r - b_g)
                                   if not whole or not upper else zeros_grp)
                res = _dot_nt(jnp.concatenate(q_parts, axis=0).astype(BF16),
                              jnp.concatenate(k_parts, axis=0).astype(BF16))
                acc = jnp.where(pair_code >= m, res, acc)
            amat_buf[par, h] = acc.astype(BF16)

    def tail(c, par):
        rs = rows(c)
        o_s = o_buf.at[par]
        for h, cs in enumerate(heads):
            o = o_s[:, cs] + _dot(amat_buf[par, h], v_ref[rs, cs])
            ms = jnp.mean(o * o, axis=-1, keepdims=True)
            o = o * lax.rsqrt(ms + EPS) * ng_ref[:, cs] * g_ref[rs, cs].astype(F32)
            o_ref[rs, cs] = o.astype(o_ref.dtype)

    _pipeline3(n_chunks, prep, front, tail)


def hg_chunk(q, k, v, lf, g, norm_g, lblk):
    bsz, s, d = q.shape
    n_heads = d // HG_D
    spec = pl.BlockSpec((None, lblk, d), lambda b, c: (b, c, 0))
    return pl.pallas_call(
        _hg_chunk_kernel,
        grid=(bsz, s // lblk),
        in_specs=[spec, spec, spec, spec, spec, pl.BlockSpec((1, d), lambda b, c: (0, 0))],
        out_specs=spec,
        out_shape=jax.ShapeDtypeStruct((bsz, s, d), BF16),
        scratch_shapes=[pltpu.VMEM((n_heads, HG_D, HG_D), F32),
                        pltpu.VMEM((2, CHUNK, d), F32),
                        pltpu.VMEM((2, CHUNK, d), F32),
                        pltpu.VMEM((2, CHUNK, d), F32),
                        pltpu.VMEM((2, CHUNK, d), F32),
                        pltpu.VMEM((2, n_heads, CHUNK, CHUNK), BF16)],
        compiler_params=_cparams("parallel", "arbitrary"),
        name="hg_chunk",
    )(q, k, v, lf, g, norm_g.reshape(1, d))


def _ml_inproj_kernel(n_lin, tn, a_ref, w_ref, wg_ref, bg_ref, p_ref, gt_ref):
    a = a_ref[...]
    for j in range(w_ref.shape[1] // tn):
        cols = slice(j * tn, (j + 1) * tn)
        acc = _dot(a, w_ref[:, cols])
        p_ref[:, cols] = (acc if j < n_lin else _sigmoid(acc)).astype(BF16)
    gts = _dot(a, wg_ref[...]) + bg_ref[...]
    lane = lax.broadcasted_iota(jnp.int32, gts.shape, 1)
    log_sig = jnp.minimum(gts, 0.0) - jnp.log(1.0 + jnp.exp(-jnp.abs(gts)))
    gt_ref[...] = jnp.where(lane < ML_HEADS, gts, log_sig)


def ml_inproj(a, w_main, w_gate, b_gate, tm, tn, n_lin):
    t, d = a.shape
    n = w_main.shape[1]
    gl = w_gate.shape[1]
    return pl.pallas_call(
        functools.partial(_ml_inproj_kernel, n_lin, tn),
        grid=(t // tm,),
        in_specs=[pl.BlockSpec((tm, d), lambda i: (i, 0)),
                  _whole(w_main), _whole(w_gate), _whole(b_gate)],
        out_specs=[pl.BlockSpec((tm, n), lambda i: (i, 0)),
                   pl.BlockSpec((tm, gl), lambda i: (i, 0))],
        out_shape=[jax.ShapeDtypeStruct((t, n), BF16),
                   jax.ShapeDtypeStruct((t, gl), F32)],
        compiler_params=_cparams("parallel"),
        name="ml_inproj",
    )(a, w_main, w_gate, b_gate)


def _ml_chunk_kernel(qk_ref, v_ref, og_ref, gc_ref, gr_ref, ng_ref, o_ref,
                     c_ref, m_ref):
    lblk = v_ref.shape[0]
    nqk = ML_HEADS * ML_DQK

    @pl.when(pl.program_id(1) == 0)
    def _():
        c_ref[...] = jnp.zeros_like(c_ref)
        m_ref[...] = jnp.zeros_like(m_ref)

    row = lax.broadcasted_iota(jnp.int32, (CHUNK, CHUNK), 0)
    col = lax.broadcasted_iota(jnp.int32, (CHUNK, CHUNK), 1)
    causal = row >= col
    tril = causal.astype(BF16)
    triu = (row <= col).astype(BF16)
    ones_v = jnp.ones((CHUNK, ML_DV), BF16)
    scale = ML_DQK ** -0.5

    def chunk_body(c, carry):
        r0 = pl.multiple_of(c * CHUNK, CHUNK)
        rs = pl.ds(r0, CHUNK)
        g_cols = gc_ref[rs, :]
        g_rows = gr_ref[c]
        b_cols = _tri_dot(tril, g_cols)
        b_rows = _dot_tri(g_rows, triu)
        heads = range(ML_HEADS)
        qs = [qk_ref[rs, h * ML_DQK:(h + 1) * ML_DQK] for h in heads]
        ks = [qk_ref[rs, nqk + h * ML_DQK:nqk + (h + 1) * ML_DQK] for h in heads]
        v_exts = [jnp.concatenate([v_ref[rs, h * ML_DV:(h + 1) * ML_DV], ones_v], axis=1)
                  for h in heads]
        c_prevs = [c_ref[h] for h in heads]
        b_reps = [jnp.broadcast_to(b_cols[:, ML_HEADS + h:ML_HEADS + h + 1], (CHUNK, ML_DV))
                  for h in heads]
        li_reps = [jnp.broadcast_to(g_cols[:, h:h + 1], (CHUNK, ML_DV)) for h in heads]
        s_qk = [_dot_nt(qs[h], ks[h]) for h in heads]
        q_c = [_dot(qs[h], c_prevs[h].astype(BF16)) for h in heads]
        m_ts, d_ms, w_inters, kws, decs = [], [], [], [], []
        for h in heads:
            b_rep = b_reps[h]
            b_row = b_rows[ML_HEADS + h:ML_HEADS + h + 1, :]
            li_row = g_rows[h:h + 1, :]
            m_prev = m_ref[h:h + 1, :]
            log_d = jnp.where(causal, b_rep[:, :CHUNK] - b_row + li_row, NEG_BIG)
            inter = b_rep + m_prev
            m_t = jnp.maximum(inter, jnp.max(log_d, axis=-1, keepdims=True))
            m_ts.append(m_t)
            d_ms.append(jnp.exp(log_d - m_t[:, :CHUNK]))
            w_inters.append(jnp.exp(inter - m_t))
            b_last = b_rep[CHUNK - 1:CHUNK, :]
            log_w = b_last - b_rep + li_reps[h]
            m_new = jnp.maximum(b_last + m_prev, jnp.max(log_w, axis=0, keepdims=True))
            w = jnp.exp(log_w - m_new)
            decs.append(jnp.exp(b_last + m_prev - m_new))
            kws.append((ks[h].astype(F32) * (w[:, :ML_DQK] * scale)).astype(BF16))
            m_ref[h:h + 1, :] = m_new
        nums = []
        for h in heads:
            s_mat = (s_qk[h] * scale * d_ms[h]).astype(BF16)
            w2 = jnp.concatenate([w_inters[h], w_inters[h]], axis=1)
            nums.append(_dot(s_mat, v_exts[h]) + w2 * q_c[h])
            dec2 = jnp.concatenate([decs[h], decs[h]], axis=1)
            c_ref[h] = dec2 * c_prevs[h] + _dot_tn(kws[h], v_exts[h])
        for h in heads:
            den = nums[h][:, ML_DV:]
            hh = nums[h][:, :ML_DV] / jnp.maximum(jnp.abs(den), jnp.exp(-m_ts[h]))
            vs = pl.ds(h * ML_DV, ML_DV)
            ms = jnp.mean(hh * hh, axis=-1, keepdims=True)
            out = hh * lax.rsqrt(ms + EPS) * ng_ref[:, vs] * og_ref[rs, vs].astype(F32)
            o_ref[rs, vs] = out.astype(o_ref.dtype)
        return carry

    lax.fori_loop(0, lblk // CHUNK, chunk_body, 0)


def ml_chunk(proj, gates_cols, gates_rows, norm_g, lblk):
    bsz, s, _ = proj.shape
    d = ML_HEADS * ML_DV
    nqk2 = 2 * ML_HEADS * ML_DQK
    gl = gates_cols.shape[-1]
    return pl.pallas_call(
        _ml_chunk_kernel,
        grid=(bsz, s // lblk),
        in_specs=[pl.BlockSpec((None, lblk, nqk2), lambda b, c: (b, c, 0)),
                  pl.BlockSpec((None, lblk, d), lambda b, c: (b, c, nqk2 // d)),
                  pl.BlockSpec((None, lblk, d), lambda b, c: (b, c, nqk2 // d + 1)),
                  pl.BlockSpec((None, lblk, gl), lambda b, c: (b, c, 0)),
                  pl.BlockSpec((None, lblk // CHUNK, 2 * ML_HEADS, CHUNK),
                               lambda b, c: (b, c, 0, 0)),
                  pl.BlockSpec((1, d), lambda b, c: (0, 0))],
        out_specs=pl.BlockSpec((None, lblk, d), lambda b, c: (b, c, 0)),
        out_shape=jax.ShapeDtypeStruct((bsz, s, d), BF16),
        scratch_shapes=[pltpu.VMEM((ML_HEADS, ML_DQK, 2 * ML_DV), F32),
                        pltpu.VMEM((ML_HEADS, 128), F32)],
        compiler_params=_cparams("parallel", "arbitrary"),
        name="ml_chunk",
    )(proj, proj, proj, gates_cols, gates_rows, norm_g.reshape(1, d))


def _mem_kv_kernel(m_ref, g_ref, w_ref, kv_ref):
    mn = _rms_rows(m_ref[...], g_ref[...]).astype(BF16)
    kv_ref[...] = _dot(mn, w_ref[...]).astype(BF16)


def mem_kv(mem2d, g, wkv, tn):
    t, d = mem2d.shape
    n = wkv.shape[1]
    return pl.pallas_call(
        _mem_kv_kernel,
        grid=(n // tn,),
        in_specs=[pl.BlockSpec((t, d), lambda j: (0, 0)),
                  pl.BlockSpec((1, d), lambda j: (0, 0)),
                  pl.BlockSpec((d, tn), lambda j: (0, j))],
        out_specs=pl.BlockSpec((t, tn), lambda j: (0, j)),
        out_shape=jax.ShapeDtypeStruct((t, n), BF16),
        compiler_params=_cparams("parallel"),
        name="mem_kv",
    )(mem2d, g.reshape(1, d), wkv)


def _xattn_kernel(om_ref, wm_ref, gx_ref, k_ref, v_ref, wq_ref, wo_ref, h_ref, g_ref,
                  hn_ref, an_ref):
    d = om_ref.shape[1]
    hd = d // XA_HEADS
    scale = hd ** -0.5
    h_mix = h_ref[...] + _dot(om_ref[...], wm_ref[...])
    a = _rms_rows(h_mix, gx_ref[...]).astype(BF16)
    q = _dot(a, wq_ref[...]).astype(BF16)
    outs = []
    for hh in range(XA_HEADS):
        cs = slice(hh * hd, (hh + 1) * hd)
        s = _dot_nt(q[:, cs], k_ref[:, cs]) * scale
        p = jnp.exp(s - jnp.max(s, axis=-1, keepdims=True))
        p = p / jnp.sum(p, axis=-1, keepdims=True)
        outs.append(_dot(p.astype(BF16), v_ref[:, cs]).astype(BF16))
    o = jnp.concatenate(outs, axis=1)
    hn = h_mix + _dot(o, wo_ref[...])
    hn_ref[...] = hn
    an_ref[...] = _rms_rows(hn, g_ref[...]).astype(an_ref.dtype)


def mix_xattn_res_norm(o_mix, w_mix, g_xa, kv, wq, wo, h, g_next, tm):
    bsz, s, d = o_mix.shape
    n_mem = kv.shape[1]
    tok = pl.BlockSpec((None, tm, d), lambda b, i: (b, i, 0))
    g_xa, g_next = g_xa.reshape(1, d), g_next.reshape(1, d)
    return pl.pallas_call(
        _xattn_kernel,
        grid=(bsz, s // tm),
        in_specs=[tok, _whole(w_mix), _whole(g_xa),
                  pl.BlockSpec((None, n_mem, d), lambda b, i: (b, 0, 0)),
                  pl.BlockSpec((None, n_mem, d), lambda b, i: (b, 0, 1)),
                  _whole(wq), _whole(wo), tok, _whole(g_next)],
        out_specs=[tok, tok],
        out_shape=[jax.ShapeDtypeStruct((bsz, s, d), F32),
                   jax.ShapeDtypeStruct((bsz, s, d), BF16)],
        compiler_params=_cparams("parallel", "parallel"),
        name="mix_xattn_res_norm",
    )(o_mix, w_mix, g_xa, kv, kv, wq, wo, h, g_next)


def _ffn_kernel(tf, a_ref, ah_ref, wup_ref, cw_ref, cb_ref, wd_ref, h_ref, g_ref,
                hn_ref, an_ref, u_s, y_s):
    i = pl.program_id(1)
    tm = a_ref.shape[0]
    dff = wd_ref.shape[0]
    n_lane = tf // LANES

    halo = jnp.where(i > 0, ah_ref[...], jnp.zeros_like(ah_ref))
    a_ext = jnp.concatenate([halo, a_ref[...]], axis=0)

    for j in range(dff // tf):
        buf = j % 2
        for part in range(2):
            col = part * dff + j * tf
            u = _dot(a_ext, wup_ref[:, col:col + tf])
            for c in range(n_lane):
                u_s[buf, part, c] = u[:, c * LANES:(c + 1) * LANES]
        for c in range(n_lane):
            conv = []
            for part in range(2):
                lanes = slice(part * dff + j * tf + c * LANES,
                              part * dff + j * tf + (c + 1) * LANES)
                out = cb_ref[:, lanes]
                for tap in range(CONV_W):
                    sh = CONV_W - 1 - tap
                    out = out + (u_s[buf, part, c, HALO - sh:HALO - sh + tm, :]
                                 * cw_ref[tap:tap + 1, lanes])
                conv.append(out)
            gate, val = conv
            y_s[:, j * tf + c * LANES:j * tf + (c + 1) * LANES] = (
                gate * _sigmoid(gate) * val).astype(BF16)

    hn = h_ref[...] + _dot(y_s[...], wd_ref[...])
    hn_ref[...] = hn
    an_ref[...] = _rms_rows(hn, g_ref[...]).astype(an_ref.dtype)


def ffn_res_norm(a, w_up, conv_w, conv_b, w_down, h, g_next, a_dtype, tm, tf):
    bsz, s, d = a.shape
    dff = w_down.shape[0]
    hb = tm // HALO
    tok = pl.BlockSpec((None, tm, d), lambda b, i: (b, i, 0))
    conv_b = conv_b.reshape(1, -1)
    g_next = g_next.reshape(1, d)
    return pl.pallas_call(
        functools.partial(_ffn_kernel, tf),
        grid=(bsz, s // tm),
        in_specs=[tok,
                  pl.BlockSpec((None, HALO, d),
                               lambda b, i: (b, jnp.maximum(i * hb - 1, 0), 0)),
                  _whole(w_up), _whole(conv_w), _whole(conv_b), _whole(w_down),
                  tok, _whole(g_next)],
        out_specs=[tok, tok],
        out_shape=[jax.ShapeDtypeStruct((bsz, s, d), F32),
                   jax.ShapeDtypeStruct((bsz, s, d), a_dtype)],
        scratch_shapes=[pltpu.VMEM((2, 2, tf // LANES, HALO + tm, LANES), F32),
                        pltpu.VMEM((tm, dff), BF16)],
        compiler_params=_cparams("parallel", "parallel"),
        name="ffn_res_norm",
    )(a, a, w_up, conv_w, conv_b, w_down, h, g_next)


def _pick(n, pref):
    for c in pref:
        if n % c == 0:
            return c
    return n


def kernel(x, mem, norm_mix_g, norm_xa_g, norm_mem_g, norm_ffn_g, hg_w_in, hg_w_out, hg_norm_g, hg_lb_logits, ml_w_in, ml_b_gate, ml_w_out, ml_norm_g, xa_wq, xa_wkv, xa_wo, ffn_w_up, ffn_conv_w, ffn_conv_b, ffn_w_down, final_g):
    bsz, s, d = x.shape
    n_mem = mem.shape[1]
    depth = norm_mix_g.shape[0]
    t = bsz * s
    tm = _pick(s, (512, 256, 128, 64))
    lblk = _pick(s, (512, 256, 128, 64))
    tn = 256
    dff = ffn_w_down.shape[1]
    tf = _pick(dff, (256, 128))
    nqk2 = 2 * ML_HEADS * ML_DQK
    n_main = nqk2 + 2 * d
    gl = LANES

    h = x
    a = x.reshape(t, d)
    mem2d = mem.reshape(bsz * n_mem, d)
    r3 = lambda z: z.reshape(bsz, s, -1)

    for layer in range(depth):
        j = layer // 2
        if layer % 2 == 0:
            q, k, lf, v, g = hg_inproj(a, norm_mix_g[layer], hg_w_in[j].astype(BF16),
                                       hg_lb_logits, layer, tm, tn)
            o = hg_chunk(r3(q), r3(k), r3(v), r3(lf), r3(g), hg_norm_g[j], lblk)
            w_out = hg_w_out[j]
        else:
            w_in = ml_w_in[j]
            w_gate = jnp.pad(w_in[:, n_main:], ((0, 0), (0, gl - 2 * ML_HEADS))).astype(BF16)
            b_gate = jnp.pad(ml_b_gate[j], (0, gl - 2 * ML_HEADS)).reshape(1, gl)
            proj, gts = ml_inproj(a, w_in[:, :n_main].astype(BF16), w_gate, b_gate,
                                  tm, tn, (nqk2 + d) // tn)
            gts = r3(gts)
            g_rows = gts[:, :, :2 * ML_HEADS].reshape(bsz, s // CHUNK, CHUNK, 2 * ML_HEADS)
            g_rows = g_rows.transpose(0, 1, 3, 2)
            o = ml_chunk(r3(proj), gts, g_rows, ml_norm_g[j], lblk)
            w_out = ml_w_out[j]

        kv = mem_kv(mem2d, norm_mem_g[layer], xa_wkv[layer].astype(BF16), 512)
        h, a = mix_xattn_res_norm(o, w_out.astype(BF16), norm_xa_g[layer],
                                  kv.reshape(bsz, n_mem, 2 * d),
                                  xa_wq[layer].astype(BF16), xa_wo[layer].astype(BF16),
                                  h, norm_ffn_g[layer], tm)

        last = layer == depth - 1
        g_next = final_g if last else norm_mix_g[layer + 1]
        h, a = ffn_res_norm(a, ffn_w_up[layer].astype(BF16), ffn_conv_w[layer],
                            ffn_conv_b[layer], ffn_w_down[layer].astype(BF16), h, g_next,
                            F32 if last else BF16, tm, tf)
        a = a.reshape(t, d) if not last else a

    return a.reshape(bsz, s, d)
```

```python
import functools

import jax
import jax.numpy as jnp
from jax import lax
from jax.experimental import pallas as pl
from jax.experimental.pallas import tpu as pltpu

F32 = jnp.float32
BF16 = jnp.bfloat16

EPS = 1e-6
NEG_BIG = -1e30
F_FLOOR = 1e-20
LOG2E = 1.4426950408889634
LANES = 128
CHUNK = 64
HG_D = 128
ML_HEADS = 8
ML_DQK = 64
ML_DV = 128
XA_HEADS = 4
CONV_W = 3
HALO = 16

VMEM_LIMIT = 56 * 1024 * 1024


def _cparams(*sem):
    return pltpu.CompilerParams(dimension_semantics=sem, vmem_limit_bytes=VMEM_LIMIT)


def _dot(a, b):
    return jnp.dot(a, b, preferred_element_type=F32)


def _dot_nt(a, b):
    return lax.dot_general(a, b, (((1,), (1,)), ((), ())), preferred_element_type=F32)


def _dot_tn(a, b):
    return lax.dot_general(a, b, (((0,), (0,)), ((), ())), preferred_element_type=F32)


def _split3(x):
    hi = x.astype(BF16)
    rest = x - hi.astype(F32)
    mid = rest.astype(BF16)
    lo = (rest - mid.astype(F32)).astype(BF16)
    return hi, mid, lo


def _tri_dot(tri, x):
    hi, mid, lo = _split3(x)
    return (_dot(tri, lo) + _dot(tri, mid)) + _dot(tri, hi)


def _dot_tri(x, tri):
    hi, mid, lo = _split3(x)
    return (_dot(lo, tri) + _dot(mid, tri)) + _dot(hi, tri)


def _sigmoid(x):
    return 0.5 * jnp.tanh(0.5 * x) + 0.5


def _silu(x):
    return x * _sigmoid(x)


def _sigmoid_pair(x):
    e = jnp.exp(-jnp.abs(x))
    big = 1.0 / (1.0 + e)
    small = e * big
    pos = x >= 0.0
    return jnp.where(pos, big, small), jnp.where(pos, small, big)


def _whole(arr):
    nd = arr.ndim
    return pl.BlockSpec(arr.shape, lambda *_: (0,) * nd, pipeline_mode=pl.Buffered(1))


def _pipeline3(n, prep, front, tail):
    prep(0, 0)
    if n == 1:
        front(0, 0)
        tail(0, 0)
        return
    assert n % 2 == 0
    prep(1, 1)
    front(0, 0)

    def two_steps(j, carry):
        for par in (1, 0):
            i = 2 * j + 2 - par
            tail(i - 1, 1 - par)
            prep(i + 1, 1 - par)
            front(i, par)
        return carry

    lax.fori_loop(0, (n - 2) // 2, two_steps, 0)
    tail(n - 2, 0)
    front(n - 1, 1)
    tail(n - 1, 1)


def _rms_rows(x, g):
    ms = jnp.mean(x * x, axis=-1, keepdims=True)
    return x * lax.rsqrt(ms + EPS) * g


def _hg_inproj_kernel(layer, tn, a_ref, gin_ref, w_ref, lbl_ref,
                      q_ref, k_ref, lf_ref, v_ref, g_ref):
    a = a_ref[...]
    if a.dtype != BF16:
        a = _rms_rows(a, gin_ref[...]).astype(BF16)
    d = a.shape[1]
    depth = lbl_ref.shape[0]
    for j in range(d // tn):
        cols = slice(j * tn, (j + 1) * tn)
        proj = lambda grp: _dot(a, w_ref[:, grp * d + j * tn:grp * d + (j + 1) * tn])
        q_ref[:, cols] = _silu(proj(0)).astype(BF16)
        v_ref[:, cols] = proj(2).astype(BF16)
        g_ref[:, cols] = _silu(proj(3)).astype(BF16)
        rows = [lbl_ref[l:l + 1, cols] for l in range(depth)]
        mx = functools.reduce(jnp.maximum, rows)
        es = [jnp.exp(r - mx) for r in rows]
        tot = functools.reduce(lambda x, y: x + y, es)
        ps = [e / tot for e in es]
        lb = functools.reduce(lambda x, y: x + y, ps[:layer + 1]) - ps[0]
        sig_pos, sig_neg = _sigmoid_pair(proj(1))
        f = lb + (1.0 - lb) * sig_pos
        lf_ref[:, cols] = jnp.log(jnp.maximum(f, F_FLOOR))
        k_ref[:, cols] = ((1.0 - lb) * sig_neg).astype(BF16)


def hg_inproj(a, g_in, w_in, lb_logits, layer, tm, tn):
    t, d = a.shape
    tok = pl.BlockSpec((tm, d), lambda i: (i, 0))
    g_in = g_in.reshape(1, d)
    return pl.pallas_call(
        functools.partial(_hg_inproj_kernel, layer, tn),
        grid=(t // tm,),
        in_specs=[tok, _whole(g_in), _whole(w_in), _whole(lb_logits)],
        out_specs=[tok] * 5,
        out_shape=[jax.ShapeDtypeStruct((t, d), BF16),
                   jax.ShapeDtypeStruct((t, d), BF16),
                   jax.ShapeDtypeStruct((t, d), F32),
                   jax.ShapeDtypeStruct((t, d), BF16),
                   jax.ShapeDtypeStruct((t, d), BF16)],
        compiler_params=_cparams("parallel"),
        name="hg_inproj",
    )(a, g_in, w_in, lb_logits)


def _hg_chunk_kernel(q_ref, k_ref, v_ref, lf_ref, g_ref, ng_ref, o_ref,
                     st_ref, b_buf, q_buf, k_buf, o_buf, amat_buf):
    lblk, d = q_ref.shape
    n_chunks = lblk // CHUNK
    n_grp = CHUNK // 8
    heads = [slice(h * HG_D, (h + 1) * HG_D) for h in range(d // HG_D)]

    @pl.when(pl.program_id(1) == 0)
    def _():
        st_ref[...] = jnp.zeros_like(st_ref)

    t_idx = lax.broadcasted_iota(jnp.int32, (CHUNK, CHUNK), 0)
    s_idx = lax.broadcasted_iota(jnp.int32, (CHUNK, CHUNK), 1)
    tril = (t_idx >= s_idx).astype(BF16)
    pair_code = jnp.where(t_idx >= s_idx, t_idx ^ s_idx, -1)
    grp_row = lax.broadcasted_iota(jnp.int32, (8, HG_D), 0)
    zeros_grp = jnp.zeros((8, HG_D), F32)
    splits = [1 << p for p in range((CHUNK - 1).bit_length())]

    def rows(c):
        return pl.ds(pl.multiple_of(c * CHUNK, CHUNK), CHUNK)

    def split_ref(b_s, g, m, cs):
        if 2 * m >= 8:
            r = (8 * g) // (2 * m) * (2 * m) + m - 1
            return b_s[pl.ds(r, 1), cs]
        tile = b_s[pl.ds(8 * g + m - 1, 1), cs]
        for blk in range(1, 8 // (2 * m)):
            tile = jnp.where(grp_row >= blk * 2 * m,
                             b_s[pl.ds(8 * g + blk * 2 * m + m - 1, 1), cs], tile)
        return tile

    def prep(c, par):
        rs = rows(c)
        b_buf[par] = _tri_dot(tril, lf_ref[rs, :]) * LOG2E
        q_buf[par] = q_ref[rs, :].astype(F32)
        k_buf[par] = k_ref[rs, :].astype(F32)

    def front(c, par):
        rs = rows(c)
        b_s, q_s, k_s, o_s = b_buf.at[par], q_buf.at[par], k_buf.at[par], o_buf.at[par]
        vb = v_ref[rs, :]
        b = b_s[...]
        b_last = b_s[CHUNK - 1:CHUNK, :]
        qb = (q_s[...] * jnp.exp2(b)).astype(BF16)
        k_dec = (k_s[...] * jnp.exp2(b_last - b)).astype(BF16)
        dec = jnp.exp2(b_last)
        for h, cs in enumerate(heads):
            st = st_ref[h]
            o_s[:, cs] = _dot_nt(qb[:, cs], st.astype(BF16))
            st_ref[h] = st * dec[:, cs] + _dot_tn(vb[:, cs], k_dec[:, cs])

        for h, cs in enumerate(heads):
            acc = jnp.where(pair_code == 0, _dot_nt(q_ref[rs, cs], k_ref[rs, cs]), 0.0)
            for m in splits:
                q_parts, k_parts = [], []
                for g in range(n_grp):
                    ts = slice(8 * g, 8 * g + 8)
                    whole = 2 * m >= 16
                    upper = (8 * g) % (2 * m) >= m
                    b_g, b_r = b_s[ts, cs], split_ref(b_s, g, m, cs)
                    q_parts.append(q_s[ts, cs] * jnp.exp2(b_g - b_r)
                                   if not whole or upper else zeros_grp)
                    k_parts.append(k_s[ts, cs] * jnp.exp2(b_r - b_g)
                                   if not whole or not upper else zeros_grp)
                res = _dot_nt(jnp.concatenate(q_parts, axis=0).astype(BF16),
                              jnp.concatenate(k_parts, axis=0).astype(BF16))
                acc = jnp.where(pair_code >= m, res, acc)
            amat_buf[par, h] = acc.astype(BF16)

    def tail(c, par):
        rs = rows(c)
        o_s = o_buf.at[par]
        for h, cs in enumerate(heads):
            o = o_s[:, cs] + _dot(amat_buf[par, h], v_ref[rs, cs])
            ms = jnp.mean(o * o, axis=-1, keepdims=True)
            o = o * lax.rsqrt(ms + EPS) * ng_ref[:, cs] * g_ref[rs, cs].astype(F32)
            o_ref[rs, cs] = o.astype(o_ref.dtype)

    _pipeline3(n_chunks, prep, front, tail)


def hg_chunk(q, k, v, lf, g, norm_g, lblk):
    bsz, s, d = q.shape
    n_heads = d // HG_D
    spec = pl.BlockSpec((None, lblk, d), lambda b, c: (b, c, 0))
    return pl.pallas_call(
        _hg_chunk_kernel,
        grid=(bsz, s // lblk),
        in_specs=[spec, spec, spec, spec, spec, pl.BlockSpec((1, d), lambda b, c: (0, 0))],
        out_specs=spec,
        out_shape=jax.ShapeDtypeStruct((bsz, s, d), BF16),
        scratch_shapes=[pltpu.VMEM((n_heads, HG_D, HG_D), F32),
                        pltpu.VMEM((2, CHUNK, d), F32),
                        pltpu.VMEM((2, CHUNK, d), F32),
                        pltpu.VMEM((2, CHUNK, d), F32),
                        pltpu.VMEM((2, CHUNK, d), F32),
                        pltpu.VMEM((2, n_heads, CHUNK, CHUNK), BF16)],
        compiler_params=_cparams("parallel", "arbitrary"),
        name="hg_chunk",
    )(q, k, v, lf, g, norm_g.reshape(1, d))


def _ml_inproj_kernel(n_lin, tn, a_ref, w_ref, wg_ref, bg_ref, p_ref, gt_ref):
    a = a_ref[...]
    for j in range(w_ref.shape[1] // tn):
        cols = slice(j * tn, (j + 1) * tn)
        acc = _dot(a, w_ref[:, cols])
        p_ref[:, cols] = (acc if j < n_lin else _sigmoid(acc)).astype(BF16)
    gts = _dot(a, wg_ref[...]) + bg_ref[...]
    lane = lax.broadcasted_iota(jnp.int32, gts.shape, 1)
    log_sig = jnp.minimum(gts, 0.0) - jnp.log(1.0 + jnp.exp(-jnp.abs(gts)))
    gt_ref[...] = jnp.where(lane < ML_HEADS, gts, log_sig)


def ml_inproj(a, w_main, w_gate, b_gate, tm, tn, n_lin):
    t, d = a.shape
    n = w_main.shape[1]
    gl = w_gate.shape[1]
    return pl.pallas_call(
        functools.partial(_ml_inproj_kernel, n_lin, tn),
        grid=(t // tm,),
        in_specs=[pl.BlockSpec((tm, d), lambda i: (i, 0)),
                  _whole(w_main), _whole(w_gate), _whole(b_gate)],
        out_specs=[pl.BlockSpec((tm, n), lambda i: (i, 0)),
                   pl.BlockSpec((tm, gl), lambda i: (i, 0))],
        out_shape=[jax.ShapeDtypeStruct((t, n), BF16),
                   jax.ShapeDtypeStruct((t, gl), F32)],
        compiler_params=_cparams("parallel"),
        name="ml_inproj",
    )(a, w_main, w_gate, b_gate)


def _ml_chunk_kernel(qk_ref, v_ref, og_ref, gc_ref, gr_ref, ng_ref, o_ref,
                     c_ref, m_ref):
    lblk = v_ref.shape[0]
    nqk = ML_HEADS * ML_DQK

    @pl.when(pl.program_id(1) == 0)
    def _():
        c_ref[...] = jnp.zeros_like(c_ref)
        m_ref[...] = jnp.zeros_like(m_ref)

    row = lax.broadcasted_iota(jnp.int32, (CHUNK, CHUNK), 0)
    col = lax.broadcasted_iota(jnp.int32, (CHUNK, CHUNK), 1)
    causal = row >= col
    tril = causal.astype(BF16)
    triu = (row <= col).astype(BF16)
    ones_v = jnp.ones((CHUNK, ML_DV), BF16)
    scale = ML_DQK ** -0.5

    def chunk_body(c, carry):
        r0 = pl.multiple_of(c * CHUNK, CHUNK)
        rs = pl.ds(r0, CHUNK)
        g_cols = gc_ref[rs, :]
        g_rows = gr_ref[c]
        b_cols = _tri_dot(tril, g_cols)
        b_rows = _dot_tri(g_rows, triu)
        heads = range(ML_HEADS)
        qs = [qk_ref[rs, h * ML_DQK:(h + 1) * ML_DQK] for h in heads]
        ks = [qk_ref[rs, nqk + h * ML_DQK:nqk + (h + 1) * ML_DQK] for h in heads]
        v_exts = [jnp.concatenate([v_ref[rs, h * ML_DV:(h + 1) * ML_DV], ones_v], axis=1)
                  for h in heads]
        c_prevs = [c_ref[h] for h in heads]
        b_reps = [jnp.broadcast_to(b_cols[:, ML_HEADS + h:ML_HEADS + h + 1], (CHUNK, ML_DV))
                  for h in heads]
        li_reps = [jnp.broadcast_to(g_cols[:, h:h + 1], (CHUNK, ML_DV)) for h in heads]
        s_qk = [_dot_nt(qs[h], ks[h]) for h in heads]
        q_c = [_dot(qs[h], c_prevs[h].astype(BF16)) for h in heads]
        m_ts, d_ms, w_inters, kws, decs = [], [], [], [], []
        for h in heads:
            b_rep = b_reps[h]
            b_row = b_rows[ML_HEADS + h:ML_HEADS + h + 1, :]
            li_row = g_rows[h:h + 1, :]
            m_prev = m_ref[h:h + 1, :]
            log_d = jnp.where(causal, b_rep[:, :CHUNK] - b_row + li_row, NEG_BIG)
            inter = b_rep + m_prev
            m_t = jnp.maximum(inter, jnp.max(log_d, axis=-1, keepdims=True))
            m_ts.append(m_t)
            d_ms.append(jnp.exp(log_d - m_t[:, :CHUNK]))
            w_inters.append(jnp.exp(inter - m_t))
            b_last = b_rep[CHUNK - 1:CHUNK, :]
            log_w = b_last - b_rep + li_reps[h]
            m_new = jnp.maximum(b_last + m_prev, jnp.max(log_w, axis=0, keepdims=True))
            w = jnp.exp(log_w - m_new)
            decs.append(jnp.exp(b_last + m_prev - m_new))
            kws.append((ks[h].astype(F32) * (w[:, :ML_DQK] * scale)).astype(BF16))
            m_ref[h:h + 1, :] = m_new
        nums = []
        for h in heads:
            s_mat = (s_qk[h] * scale * d_ms[h]).astype(BF16)
            w2 = jnp.concatenate([w_inters[h], w_inters[h]], axis=1)
            nums.append(_dot(s_mat, v_exts[h]) + w2 * q_c[h])
            dec2 = jnp.concatenate([decs[h], decs[h]], axis=1)
            c_ref[h] = dec2 * c_prevs[h] + _dot_tn(kws[h], v_exts[h])
        for h in heads:
            den = nums[h][:, ML_DV:]
            hh = nums[h][:, :ML_DV] / jnp.maximum(jnp.abs(den), jnp.exp(-m_ts[h]))
            vs = pl.ds(h * ML_DV, ML_DV)
            ms = jnp.mean(hh * hh, axis=-1, keepdims=True)
            out = hh * lax.rsqrt(ms + EPS) * ng_ref[:, vs] * og_ref[rs, vs].astype(F32)
            o_ref[rs, vs] = out.astype(o_ref.dtype)
        return carry

    lax.fori_loop(0, lblk // CHUNK, chunk_body, 0)


def ml_chunk(proj, gates_cols, gates_rows, norm_g, lblk):
    bsz, s, _ = proj.shape
    d = ML_HEADS * ML_DV
    nqk2 = 2 * ML_HEADS * ML_DQK
    gl = gates_cols.shape[-1]
    return pl.pallas_call(
        _ml_chunk_kernel,
        grid=(bsz, s // lblk),
        in_specs=[pl.BlockSpec((None, lblk, nqk2), lambda b, c: (b, c, 0)),
                  pl.BlockSpec((None, lblk, d), lambda b, c: (b, c, nqk2 // d)),
                  pl.BlockSpec((None, lblk, d), lambda b, c: (b, c, nqk2 // d + 1)),
                  pl.BlockSpec((None, lblk, gl), lambda b, c: (b, c, 0)),
                  pl.BlockSpec((None, lblk // CHUNK, 2 * ML_HEADS, CHUNK),
                               lambda b, c: (b, c, 0, 0)),
                  pl.BlockSpec((1, d), lambda b, c: (0, 0))],
        out_specs=pl.BlockSpec((None, lblk, d), lambda b, c: (b, c, 0)),
        out_shape=jax.ShapeDtypeStruct((bsz, s, d), BF16),
        scratch_shapes=[pltpu.VMEM((ML_HEADS, ML_DQK, 2 * ML_DV), F32),
                        pltpu.VMEM((ML_HEADS, 128), F32)],
        compiler_params=_cparams("parallel", "arbitrary"),
        name="ml_chunk",
    )(proj, proj, proj, gates_cols, gates_rows, norm_g.reshape(1, d))


def _mem_kv_kernel(m_ref, g_ref, w_ref, kv_ref):
    mn = _rms_rows(m_ref[...], g_ref[...]).astype(BF16)
    kv_ref[...] = _dot(mn, w_ref[...]).astype(BF16)


def mem_kv(mem2d, g, wkv, tn):
    t, d = mem2d.shape
    n = wkv.shape[1]
    return pl.pallas_call(
        _mem_kv_kernel,
        grid=(n // tn,),
        in_specs=[pl.BlockSpec((t, d), lambda j: (0, 0)),
                  pl.BlockSpec((1, d), lambda j: (0, 0)),
                  pl.BlockSpec((d, tn), lambda j: (0, j))],
        out_specs=pl.BlockSpec((t, tn), lambda j: (0, j)),
        out_shape=jax.ShapeDtypeStruct((t, n), BF16),
        compiler_params=_cparams("parallel"),
        name="mem_kv",
    )(mem2d, g.reshape(1, d), wkv)


def _xattn_kernel(om_ref, wm_ref, gx_ref, k_ref, v_ref, wq_ref, wo_ref, h_ref, g_ref,
                  hn_ref, an_ref):
    d = om_ref.shape[1]
    hd = d // XA_HEADS
    scale = hd ** -0.5
    h_mix = h_ref[...] + _dot(om_ref[...], wm_ref[...])
    a = _rms_rows(h_mix, gx_ref[...]).astype(BF16)
    q = _dot(a, wq_ref[...]).astype(BF16)
    outs = []
    for hh in range(XA_HEADS):
        cs = slice(hh * hd, (hh + 1) * hd)
        s = _dot_nt(q[:, cs], k_ref[:, cs]) * scale
        p = jnp.exp(s - jnp.max(s, axis=-1, keepdims=True))
        p = p / jnp.sum(p, axis=-1, keepdims=True)
        outs.append(_dot(p.astype(BF16), v_ref[:, cs]).astype(BF16))
    o = jnp.concatenate(outs, axis=1)
    hn = h_mix + _dot(o, wo_ref[...])
    hn_ref[...] = hn
    an_ref[...] = _rms_rows(hn, g_ref[...]).astype(an_ref.dtype)


def mix_xattn_res_norm(o_mix, w_mix, g_xa, kv, wq, wo, h, g_next, tm):
    bsz, s, d = o_mix.shape
    n_mem = kv.shape[1]
    tok = pl.BlockSpec((None, tm, d), lambda b, i: (b, i, 0))
    g_xa, g_next = g_xa.reshape(1, d), g_next.reshape(1, d)
    return pl.pallas_call(
        _xattn_kernel,
        grid=(bsz, s // tm),
        in_specs=[tok, _whole(w_mix), _whole(g_xa),
                  pl.BlockSpec((None, n_mem, d), lambda b, i: (b, 0, 0)),
                  pl.BlockSpec((None, n_mem, d), lambda b, i: (b, 0, 1)),
                  _whole(wq), _whole(wo), tok, _whole(g_next)],
        out_specs=[tok, tok],
        out_shape=[jax.ShapeDtypeStruct((bsz, s, d), F32),
                   jax.ShapeDtypeStruct((bsz, s, d), BF16)],
        compiler_params=_cparams("parallel", "parallel"),
        name="mix_xattn_res_norm",
    )(o_mix, w_mix, g_xa, kv, kv, wq, wo, h, g_next)


def _ffn_kernel(tf, a_ref, ah_ref, wup_ref, cw_ref, cb_ref, wd_ref, h_ref, g_ref,
                hn_ref, an_ref, u_s, y_s):
    i = pl.program_id(1)
    tm = a_ref.shape[0]
    dff = wd_ref.shape[0]
    n_lane = tf // LANES

    halo = jnp.where(i > 0, ah_ref[...], jnp.zeros_like(ah_ref))
    a_ext = jnp.concatenate([halo, a_ref[...]], axis=0)

    for j in range(dff // tf):
        buf = j % 2
        for part in range(2):
            col = part * dff + j * tf
            u = _dot(a_ext, wup_ref[:, col:col + tf])
            for c in range(n_lane):
                u_s[buf, part, c] = u[:, c * LANES:(c + 1) * LANES]
        for c in range(n_lane):
            conv = []
            for part in range(2):
                lanes = slice(part * dff + j * tf + c * LANES,
                              part * dff + j * tf + (c + 1) * LANES)
                out = cb_ref[:, lanes]
                for tap in range(CONV_W):
                    sh = CONV_W - 1 - tap
                    out = out + (u_s[buf, part, c, HALO - sh:HALO - sh + tm, :]
                                 * cw_ref[tap:tap + 1, lanes])
                conv.append(out)
            gate, val = conv
            y_s[:, j * tf + c * LANES:j * tf + (c + 1) * LANES] = (
                gate * _sigmoid(gate) * val).astype(BF16)

    hn = h_ref[...] + _dot(y_s[...], wd_ref[...])
    hn_ref[...] = hn
    an_ref[...] = _rms_rows(hn, g_ref[...]).astype(an_ref.dtype)


def ffn_res_norm(a, w_up, conv_w, conv_b, w_down, h, g_next, a_dtype, tm, tf):
    bsz, s, d = a.shape
    dff = w_down.shape[0]
    hb = tm // HALO
    tok = pl.BlockSpec((None, tm, d), lambda b, i: (b, i, 0))
    conv_b = conv_b.reshape(1, -1)
    g_next = g_next.reshape(1, d)
    return pl.pallas_call(
        functools.partial(_ffn_kernel, tf),
        grid=(bsz, s // tm),
        in_specs=[tok,
                  pl.BlockSpec((None, HALO, d),
                               lambda b, i: (b, jnp.maximum(i * hb - 1, 0), 0)),
                  _whole(w_up), _whole(conv_w), _whole(conv_b), _whole(w_down),
                  tok, _whole(g_next)],
        out_specs=[tok, tok],
        out_shape=[jax.ShapeDtypeStruct((bsz, s, d), F32),
                   jax.ShapeDtypeStruct((bsz, s, d), a_dtype)],
        scratch_shapes=[pltpu.VMEM((2, 2, tf // LANES, HALO + tm, LANES), F32),
                        pltpu.VMEM((tm, dff), BF16)],
        compiler_params=_cparams("parallel", "parallel"),
        name="ffn_res_norm",
    )(a, a, w_up, conv_w, conv_b, w_down, h, g_next)


def _pick(n, pref):
    for c in pref:
        if n % c == 0:
            return c
    return n


def kernel(x, mem, norm_mix_g, norm_xa_g, norm_mem_g, norm_ffn_g, hg_w_in, hg_w_out, hg_norm_g, hg_lb_logits, ml_w_in, ml_b_gate, ml_w_out, ml_norm_g, xa_wq, xa_wkv, xa_wo, ffn_w_up, ffn_conv_w, ffn_conv_b, ffn_w_down, final_g):
    bsz, s, d = x.shape
    n_mem = mem.shape[1]
    depth = norm_mix_g.shape[0]
    t = bsz * s
    tm = _pick(s, (512, 256, 128, 64))
    tm_proj = _pick(s, (1024, 512, 256, 128, 64))
    lblk = _pick(s, (512, 256, 128, 64))
    tn = 256
    dff = ffn_w_down.shape[1]
    tf = _pick(dff, (256, 128))
    nqk2 = 2 * ML_HEADS * ML_DQK
    n_main = nqk2 + 2 * d
    gl = LANES

    h = x
    a = x.reshape(t, d)
    mem2d = mem.reshape(bsz * n_mem, d)
    r3 = lambda z: z.reshape(bsz, s, -1)

    for layer in range(depth):
        j = layer // 2
        if layer % 2 == 0:
            q, k, lf, v, g = hg_inproj(a, norm_mix_g[layer], hg_w_in[j].astype(BF16),
                                       hg_lb_logits, layer, tm_proj, tn)
            o = hg_chunk(r3(q), r3(k), r3(v), r3(lf), r3(g), hg_norm_g[j], lblk)
            w_out = hg_w_out[j]
        else:
            w_in = ml_w_in[j]
            w_gate = jnp.pad(w_in[:, n_main:], ((0, 0), (0, gl - 2 * ML_HEADS))).astype(BF16)
            b_gate = jnp.pad(ml_b_gate[j], (0, gl - 2 * ML_HEADS)).reshape(1, gl)
            proj, gts = ml_inproj(a, w_in[:, :n_main].astype(BF16), w_gate, b_gate,
                                  tm_proj, tn, (nqk2 + d) // tn)
            gts = r3(gts)
            g_rows = gts[:, :, :2 * ML_HEADS].reshape(bsz, s // CHUNK, CHUNK, 2 * ML_HEADS)
            g_rows = g_rows.transpose(0, 1, 3, 2)
            o = ml_chunk(r3(proj), gts, g_rows, ml_norm_g[j], lblk)
            w_out = ml_w_out[j]

        kv = mem_kv(mem2d, norm_mem_g[layer], xa_wkv[layer].astype(BF16), 512)
        h, a = mix_xattn_res_norm(o, w_out.astype(BF16), norm_xa_g[layer],
                                  kv.reshape(bsz, n_mem, 2 * d),
                                  xa_wq[layer].astype(BF16), xa_wo[layer].astype(BF16),
                                  h, norm_ffn_g[layer], tm_proj)

        last = layer == depth - 1
        g_next = final_g if last else norm_mix_g[layer + 1]
        h, a = ffn_res_norm(a, ffn_w_up[layer].astype(BF16), ffn_conv_w[layer],
                            ffn_conv_b[layer], ffn_w_down[layer].astype(BF16), h, g_next,
                            F32 if last else BF16, tm, tf)
        a = a.reshape(t, d) if not last else a

    return a.reshape(bsz, s, d)
```

```python
import functools

import jax
import jax.numpy as jnp
from jax import lax
from jax.experimental import pallas as pl
from jax.experimental.pallas import tpu as pltpu

F32 = jnp.float32
BF16 = jnp.bfloat16

EPS = 1e-6
NEG_BIG = -1e30
F_FLOOR = 1e-20
LOG2E = 1.4426950408889634
LANES = 128
CHUNK = 64
HG_D = 128
ML_HEADS = 8
ML_DQK = 64
ML_DV = 128
XA_HEADS = 4
CONV_W = 3
HALO = 16

VMEM_LIMIT = 56 * 1024 * 1024


def _cparams(*sem):
    return pltpu.CompilerParams(dimension_semantics=sem, vmem_limit_bytes=VMEM_LIMIT)


def _dot(a, b):
    return jnp.dot(a, b, preferred_element_type=F32)


def _dot_nt(a, b):
    return lax.dot_general(a, b, (((1,), (1,)), ((), ())), preferred_element_type=F32)


def _dot_tn(a, b):
    return lax.dot_general(a, b, (((0,), (0,)), ((), ())), preferred_element_type=F32)


def _split3(x):
    hi = x.astype(BF16)
    rest = x - hi.astype(F32)
    mid = rest.astype(BF16)
    lo = (rest - mid.astype(F32)).astype(BF16)
    return hi, mid, lo


def _tri_dot(tri, x):
    hi, mid, lo = _split3(x)
    return (_dot(tri, lo) + _dot(tri, mid)) + _dot(tri, hi)


def _dot_tri(x, tri):
    hi, mid, lo = _split3(x)
    return (_dot(lo, tri) + _dot(mid, tri)) + _dot(hi, tri)


def _sigmoid(x):
    return 0.5 * jnp.tanh(0.5 * x) + 0.5


def _silu(x):
    return x * _sigmoid(x)


def _sigmoid_pair(x):
    e = jnp.exp(-jnp.abs(x))
    big = 1.0 / (1.0 + e)
    small = e * big
    pos = x >= 0.0
    return jnp.where(pos, big, small), jnp.where(pos, small, big)


def _whole(arr):
    nd = arr.ndim
    return pl.BlockSpec(arr.shape, lambda *_: (0,) * nd, pipeline_mode=pl.Buffered(1))


def _pipeline3(n, prep, front, tail):
    prep(0, 0)
    if n == 1:
        front(0, 0)
        tail(0, 0)
        return
    assert n % 2 == 0
    prep(1, 1)
    front(0, 0)

    def two_steps(j, carry):
        for par in (1, 0):
            i = 2 * j + 2 - par
            tail(i - 1, 1 - par)
            prep(i + 1, 1 - par)
            front(i, par)
        return carry

    lax.fori_loop(0, (n - 2) // 2, two_steps, 0)
    tail(n - 2, 0)
    front(n - 1, 1)
    tail(n - 1, 1)


def _rms_rows(x, g):
    ms = jnp.mean(x * x, axis=-1, keepdims=True)
    return x * lax.rsqrt(ms + EPS) * g


def _hg_inproj_kernel(layer, tn, a_ref, gin_ref, w_ref, lbl_ref,
                      q_ref, k_ref, lf_ref, v_ref, g_ref):
    a = a_ref[...]
    if a.dtype != BF16:
        a = _rms_rows(a, gin_ref[...]).astype(BF16)
    d = a.shape[1]
    depth = lbl_ref.shape[0]
    for j in range(d // tn):
        cols = slice(j * tn, (j + 1) * tn)
        proj = lambda grp: _dot(a, w_ref[:, grp * d + j * tn:grp * d + (j + 1) * tn])
        q_ref[:, cols] = _silu(proj(0)).astype(BF16)
        v_ref[:, cols] = proj(2).astype(BF16)
        g_ref[:, cols] = _silu(proj(3)).astype(BF16)
        rows = [lbl_ref[l:l + 1, cols] for l in range(depth)]
        mx = functools.reduce(jnp.maximum, rows)
        es = [jnp.exp(r - mx) for r in rows]
        tot = functools.reduce(lambda x, y: x + y, es)
        ps = [e / tot for e in es]
        lb = functools.reduce(lambda x, y: x + y, ps[:layer + 1]) - ps[0]
        sig_pos, sig_neg = _sigmoid_pair(proj(1))
        f = lb + (1.0 - lb) * sig_pos
        lf_ref[:, cols] = jnp.log(jnp.maximum(f, F_FLOOR))
        k_ref[:, cols] = ((1.0 - lb) * sig_neg).astype(BF16)


def hg_inproj(a, g_in, w_in, lb_logits, layer, tm, tn):
    t, d = a.shape
    tok = pl.BlockSpec((tm, d), lambda i: (i, 0))
    g_in = g_in.reshape(1, d)
    return pl.pallas_call(
        functools.partial(_hg_inproj_kernel, layer, tn),
        grid=(t // tm,),
        in_specs=[tok, _whole(g_in), _whole(w_in), _whole(lb_logits)],
        out_specs=[tok] * 5,
        out_shape=[jax.ShapeDtypeStruct((t, d), BF16),
                   jax.ShapeDtypeStruct((t, d), BF16),
                   jax.ShapeDtypeStruct((t, d), F32),
                   jax.ShapeDtypeStruct((t, d), BF16),
                   jax.ShapeDtypeStruct((t, d), BF16)],
        compiler_params=_cparams("parallel"),
        name="hg_inproj",
    )(a, g_in, w_in, lb_logits)


def _hg_chunk_kernel(q_ref, k_ref, v_ref, lf_ref, g_ref, ng_ref, o_ref,
                     st_ref, b_buf, q_buf, k_buf, o_buf, amat_buf):
    lblk, d = q_ref.shape
    n_chunks = lblk // CHUNK
    n_grp = CHUNK // 8
    heads = [slice(h * HG_D, (h + 1) * HG_D) for h in range(d // HG_D)]

    @pl.when(pl.program_id(1) == 0)
    def _():
        st_ref[...] = jnp.zeros_like(st_ref)

    t_idx = lax.broadcasted_iota(jnp.int32, (CHUNK, CHUNK), 0)
    s_idx = lax.broadcasted_iota(jnp.int32, (CHUNK, CHUNK), 1)
    tril = (t_idx >= s_idx).astype(BF16)
    pair_code = jnp.where(t_idx >= s_idx, t_idx ^ s_idx, -1)
    grp_row = lax.broadcasted_iota(jnp.int32, (8, HG_D), 0)
    zeros_grp = jnp.zeros((8, HG_D), F32)
    splits = [1 << p for p in range((CHUNK - 1).bit_length())]

    def rows(c):
        return pl.ds(pl.multiple_of(c * CHUNK, CHUNK), CHUNK)

    def split_ref(b_s, g, m, cs):
        if 2 * m >= 8:
            r = (8 * g) // (2 * m) * (2 * m) + m - 1
            return b_s[pl.ds(r, 1), cs]
        tile = b_s[pl.ds(8 * g + m - 1, 1), cs]
        for blk in range(1, 8 // (2 * m)):
            tile = jnp.where(grp_row >= blk * 2 * m,
                             b_s[pl.ds(8 * g + blk * 2 * m + m - 1, 1), cs], tile)
        return tile

    def prep(c, par):
        rs = rows(c)
        b_buf[par] = _tri_dot(tril, lf_ref[rs, :]) * LOG2E
        q_buf[par] = q_ref[rs, :].astype(F32)
        k_buf[par] = k_ref[rs, :].astype(F32)

    def front(c, par):
        rs = rows(c)
        b_s, q_s, k_s, o_s = b_buf.at[par], q_buf.at[par], k_buf.at[par], o_buf.at[par]
        vb = v_ref[rs, :]
        b = b_s[...]
        b_last = b_s[CHUNK - 1:CHUNK, :]
        qb = (q_s[...] * jnp.exp2(b)).astype(BF16)
        k_dec = (k_s[...] * jnp.exp2(b_last - b)).astype(BF16)
        dec = jnp.exp2(b_last)
        for h, cs in enumerate(heads):
            st = st_ref[h]
            o_s[:, cs] = _dot_nt(qb[:, cs], st.astype(BF16))
            st_ref[h] = st * dec[:, cs] + _dot_tn(vb[:, cs], k_dec[:, cs])

        for h, cs in enumerate(heads):
            acc = jnp.where(pair_code == 0, _dot_nt(q_ref[rs, cs], k_ref[rs, cs]), 0.0)
            for m in splits:
                q_parts, k_parts = [], []
                for g in range(n_grp):
                    ts = slice(8 * g, 8 * g + 8)
                    whole = 2 * m >= 16
                    upper = (8 * g) % (2 * m) >= m
                    b_g, b_r = b_s[ts, cs], split_ref(b_s, g, m, cs)
                    q_parts.append(q_s[ts, cs] * jnp.exp2(b_g - b_r)
                                   if not whole or upper else zeros_grp)
                    k_parts.append(k_s[ts, cs] * jnp.exp2(b_r - b_g)
                                   if not whole or not upper else zeros_grp)
                res = _dot_nt(jnp.concatenate(q_parts, axis=0).astype(BF16),
                              jnp.concatenate(k_parts, axis=0).astype(BF16))
                acc = jnp.where(pair_code >= m, res, acc)
            amat_buf[par, h] = acc.astype(BF16)

    def tail(c, par):
        rs = rows(c)
        o_s = o_buf.at[par]
        for h, cs in enumerate(heads):
            o = o_s[:, cs] + _dot(amat_buf[par, h], v_ref[rs, cs])
            ms = jnp.mean(o * o, axis=-1, keepdims=True)
            o = o * lax.rsqrt(ms + EPS) * ng_ref[:, cs] * g_ref[rs, cs].astype(F32)
            o_ref[rs, cs] = o.astype(o_ref.dtype)

    _pipeline3(n_chunks, prep, front, tail)


def hg_chunk(q, k, v, lf, g, norm_g, lblk):
    bsz, s, d = q.shape
    n_heads = d // HG_D
    spec = pl.BlockSpec((None, lblk, d), lambda b, c: (b, c, 0))
    return pl.pallas_call(
        _hg_chunk_kernel,
        grid=(bsz, s // lblk),
        in_specs=[spec, spec, spec, spec, spec, pl.BlockSpec((1, d), lambda b, c: (0, 0))],
        out_specs=spec,
        out_shape=jax.ShapeDtypeStruct((bsz, s, d), BF16),
        scratch_shapes=[pltpu.VMEM((n_heads, HG_D, HG_D), F32),
                        pltpu.VMEM((2, CHUNK, d), F32),
                        pltpu.VMEM((2, CHUNK, d), F32),
                        pltpu.VMEM((2, CHUNK, d), F32),
                        pltpu.VMEM((2, CHUNK, d), F32),
                        pltpu.VMEM((2, n_heads, CHUNK, CHUNK), BF16)],
        compiler_params=_cparams("parallel", "arbitrary"),
        name="hg_chunk",
    )(q, k, v, lf, g, norm_g.reshape(1, d))


def _ml_inproj_kernel(n_lin, tn, a_ref, w_ref, wg_ref, bg_ref, p_ref, gt_ref):
    a = a_ref[...]
    for j in range(w_ref.shape[1] // tn):
        cols = slice(j * tn, (j + 1) * tn)
        acc = _dot(a, w_ref[:, cols])
        p_ref[:, cols] = (acc if j < n_lin else _sigmoid(acc)).astype(BF16)
    gts = _dot(a, wg_ref[...]) + bg_ref[...]
    lane = lax.broadcasted_iota(jnp.int32, gts.shape, 1)
    log_sig = jnp.minimum(gts, 0.0) - jnp.log(1.0 + jnp.exp(-jnp.abs(gts)))
    gt_ref[...] = jnp.where(lane < ML_HEADS, gts, log_sig)


def ml_inproj(a, w_main, w_gate, b_gate, tm, tn, n_lin):
    t, d = a.shape
    n = w_main.shape[1]
    gl = w_gate.shape[1]
    return pl.pallas_call(
        functools.partial(_ml_inproj_kernel, n_lin, tn),
        grid=(t // tm,),
        in_specs=[pl.BlockSpec((tm, d), lambda i: (i, 0)),
                  _whole(w_main), _whole(w_gate), _whole(b_gate)],
        out_specs=[pl.BlockSpec((tm, n), lambda i: (i, 0)),
                   pl.BlockSpec((tm, gl), lambda i: (i, 0))],
        out_shape=[jax.ShapeDtypeStruct((t, n), BF16),
                   jax.ShapeDtypeStruct((t, gl), F32)],
        compiler_params=_cparams("parallel"),
        name="ml_inproj",
    )(a, w_main, w_gate, b_gate)


def _ml_chunk_kernel(qk_ref, v_ref, og_ref, gc_ref, gr_ref, ng_ref, o_ref,
                     c_ref, m_ref):
    lblk = v_ref.shape[0]
    nqk = ML_HEADS * ML_DQK

    @pl.when(pl.program_id(1) == 0)
    def _():
        c_ref[...] = jnp.zeros_like(c_ref)
        m_ref[...] = jnp.zeros_like(m_ref)

    row = lax.broadcasted_iota(jnp.int32, (CHUNK, CHUNK), 0)
    col = lax.broadcasted_iota(jnp.int32, (CHUNK, CHUNK), 1)
    causal = row >= col
    tril = causal.astype(BF16)
    triu = (row <= col).astype(BF16)
    ones_v = jnp.ones((CHUNK, ML_DV), BF16)
    scale = ML_DQK ** -0.5

    def chunk_body(c, carry):
        r0 = pl.multiple_of(c * CHUNK, CHUNK)
        rs = pl.ds(r0, CHUNK)
        g_cols = gc_ref[rs, :]
        g_rows = gr_ref[c]
        b_cols = _tri_dot(tril, g_cols)
        b_rows = _dot_tri(g_rows, triu)
        heads = range(ML_HEADS)
        qs = [qk_ref[rs, h * ML_DQK:(h + 1) * ML_DQK] for h in heads]
        ks = [qk_ref[rs, nqk + h * ML_DQK:nqk + (h + 1) * ML_DQK] for h in heads]
        v_exts = [jnp.concatenate([v_ref[rs, h * ML_DV:(h + 1) * ML_DV], ones_v], axis=1)
                  for h in heads]
        c_prevs = [c_ref[h] for h in heads]
        b_reps = [jnp.broadcast_to(b_cols[:, ML_HEADS + h:ML_HEADS + h + 1], (CHUNK, ML_DV))
                  for h in heads]
        li_reps = [jnp.broadcast_to(g_cols[:, h:h + 1], (CHUNK, ML_DV)) for h in heads]
        s_qk = [_dot_nt(qs[h], ks[h]) for h in heads]
        q_c = [_dot(qs[h], c_prevs[h].astype(BF16)) for h in heads]
        m_ts, d_ms, w_inters, kws, decs = [], [], [], [], []
        for h in heads:
            b_rep = b_reps[h]
            b_row = b_rows[ML_HEADS + h:ML_HEADS + h + 1, :]
            li_row = g_rows[h:h + 1, :]
            m_prev = m_ref[h:h + 1, :]
            log_d = jnp.where(causal, b_rep[:, :CHUNK] - b_row + li_row, NEG_BIG)
            inter = b_rep + m_prev
            m_t = jnp.maximum(inter, jnp.max(log_d, axis=-1, keepdims=True))
            m_ts.append(m_t)
            d_ms.append(jnp.exp(log_d - m_t[:, :CHUNK]))
            w_inters.append(jnp.exp(inter - m_t))
            b_last = b_rep[CHUNK - 1:CHUNK, :]
            log_w = b_last - b_rep + li_reps[h]
            m_new = jnp.maximum(b_last + m_prev, jnp.max(log_w, axis=0, keepdims=True))
            w = jnp.exp(log_w - m_new)
            decs.append(jnp.exp(b_last + m_prev - m_new))
            kws.append((ks[h].astype(F32) * (w[:, :ML_DQK] * scale)).astype(BF16))
            m_ref[h:h + 1, :] = m_new
        nums = []
        for h in heads:
            s_mat = (s_qk[h] * scale * d_ms[h]).astype(BF16)
            w2 = jnp.concatenate([w_inters[h], w_inters[h]], axis=1)
            nums.append(_dot(s_mat, v_exts[h]) + w2 * q_c[h])
            dec2 = jnp.concatenate([decs[h], decs[h]], axis=1)
            c_ref[h] = dec2 * c_prevs[h] + _dot_tn(kws[h], v_exts[h])
        for h in heads:
            den = nums[h][:, ML_DV:]
            hh = nums[h][:, :ML_DV] / jnp.maximum(jnp.abs(den), jnp.exp(-m_ts[h]))
            vs = pl.ds(h * ML_DV, ML_DV)
            ms = jnp.mean(hh * hh, axis=-1, keepdims=True)
            out = hh * lax.rsqrt(ms + EPS) * ng_ref[:, vs] * og_ref[rs, vs].astype(F32)
            o_ref[rs, vs] = out.astype(o_ref.dtype)
        return carry

    lax.fori_loop(0, lblk // CHUNK, chunk_body, 0)


def ml_chunk(proj, gates_cols, gates_rows, norm_g, lblk):
    bsz, s, _ = proj.shape
    d = ML_HEADS * ML_DV
    nqk2 = 2 * ML_HEADS * ML_DQK
    gl = gates_cols.shape[-1]
    return pl.pallas_call(
        _ml_chunk_kernel,
        grid=(bsz, s // lblk),
        in_specs=[pl.BlockSpec((None, lblk, nqk2), lambda b, c: (b, c, 0)),
                  pl.BlockSpec((None, lblk, d), lambda b, c: (b, c, nqk2 // d)),
                  pl.BlockSpec((None, lblk, d), lambda b, c: (b, c, nqk2 // d + 1)),
                  pl.BlockSpec((None, lblk, gl), lambda b, c: (b, c, 0)),
                  pl.BlockSpec((None, lblk // CHUNK, 2 * ML_HEADS, CHUNK),
                               lambda b, c: (b, c, 0, 0)),
                  pl.BlockSpec((1, d), lambda b, c: (0, 0))],
        out_specs=pl.BlockSpec((None, lblk, d), lambda b, c: (b, c, 0)),
        out_shape=jax.ShapeDtypeStruct((bsz, s, d), BF16),
        scratch_shapes=[pltpu.VMEM((ML_HEADS, ML_DQK, 2 * ML_DV), F32),
                        pltpu.VMEM((ML_HEADS, 128), F32)],
        compiler_params=_cparams("parallel", "arbitrary"),
        name="ml_chunk",
    )(proj, proj, proj, gates_cols, gates_rows, norm_g.reshape(1, d))


def _mem_kv_kernel(m_ref, g_ref, w_ref, kv_ref):
    mn = _rms_rows(m_ref[...], g_ref[...]).astype(BF16)
    kv_ref[...] = _dot(mn, w_ref[...]).astype(BF16)


def mem_kv(mem2d, g, wkv, tn):
    t, d = mem2d.shape
    n = wkv.shape[1]
    return pl.pallas_call(
        _mem_kv_kernel,
        grid=(n // tn,),
        in_specs=[pl.BlockSpec((t, d), lambda j: (0, 0)),
                  pl.BlockSpec((1, d), lambda j: (0, 0)),
                  pl.BlockSpec((d, tn), lambda j: (0, j))],
        out_specs=pl.BlockSpec((t, tn), lambda j: (0, j)),
        out_shape=jax.ShapeDtypeStruct((t, n), BF16),
        compiler_params=_cparams("parallel"),
        name="mem_kv",
    )(mem2d, g.reshape(1, d), wkv)


def _xattn_kernel(om_ref, wm_ref, gx_ref, k_ref, v_ref, wq_ref, wo_ref, h_ref, g_ref,
                  hn_ref, an_ref):
    d = om_ref.shape[1]
    hd = d // XA_HEADS
    scale = hd ** -0.5
    h_mix = h_ref[...] + _dot(om_ref[...], wm_ref[...])
    a = _rms_rows(h_mix, gx_ref[...]).astype(BF16)
    q = _dot(a, wq_ref[...]).astype(BF16)
    outs = []
    for hh in range(XA_HEADS):
        cs = slice(hh * hd, (hh + 1) * hd)
        s = _dot_nt(q[:, cs], k_ref[:, cs]) * scale
        p = jnp.exp(s - jnp.max(s, axis=-1, keepdims=True))
        p = p / jnp.sum(p, axis=-1, keepdims=True)
        outs.append(_dot(p.astype(BF16), v_ref[:, cs]).astype(BF16))
    o = jnp.concatenate(outs, axis=1)
    hn = h_mix + _dot(o, wo_ref[...])
    hn_ref[...] = hn
    an_ref[...] = _rms_rows(hn, g_ref[...]).astype(an_ref.dtype)


def mix_xattn_res_norm(o_mix, w_mix, g_xa, kv, wq, wo, h, g_next, tm):
    bsz, s, d = o_mix.shape
    n_mem = kv.shape[1]
    tok = pl.BlockSpec((None, tm, d), lambda b, i: (b, i, 0))
    g_xa, g_next = g_xa.reshape(1, d), g_next.reshape(1, d)
    return pl.pallas_call(
        _xattn_kernel,
        grid=(bsz, s // tm),
        in_specs=[tok, _whole(w_mix), _whole(g_xa),
                  pl.BlockSpec((None, n_mem, d), lambda b, i: (b, 0, 0)),
                  pl.BlockSpec((None, n_mem, d), lambda b, i: (b, 0, 1)),
                  _whole(wq), _whole(wo), tok, _whole(g_next)],
        out_specs=[tok, tok],
        out_shape=[jax.ShapeDtypeStruct((bsz, s, d), F32),
                   jax.ShapeDtypeStruct((bsz, s, d), BF16)],
        compiler_params=_cparams("parallel", "parallel"),
        name="mix_xattn_res_norm",
    )(o_mix, w_mix, g_xa, kv, kv, wq, wo, h, g_next)


def _ffn_kernel(tf, a_ref, ah_ref, wup_ref, cw_ref, cb_ref, wd_ref, h_ref, g_ref,
                hn_ref, an_ref, u_s, y_s):
    i = pl.program_id(1)
    tm = a_ref.shape[0]
    dff = wd_ref.shape[0]
    tiles = [(f0, min(tf, dff - f0)) for f0 in range(0, dff, tf)]

    halo = jnp.where(i > 0, ah_ref[...], jnp.zeros_like(ah_ref))
    a_ext = jnp.concatenate([halo, a_ref[...]], axis=0)

    for j, (f0, fw) in enumerate(tiles):
        buf = j % 2
        n_lane = fw // LANES
        for part in range(2):
            col = part * dff + f0
            u = _dot(a_ext, wup_ref[:, col:col + fw])
            for c in range(n_lane):
                u_s[buf, part, c] = u[:, c * LANES:(c + 1) * LANES]
        for c in range(n_lane):
            conv = []
            for part in range(2):
                lanes = slice(part * dff + f0 + c * LANES,
                              part * dff + f0 + (c + 1) * LANES)
                out = cb_ref[:, lanes]
                for tap in range(CONV_W):
                    sh = CONV_W - 1 - tap
                    out = out + (u_s[buf, part, c, HALO - sh:HALO - sh + tm, :]
                                 * cw_ref[tap:tap + 1, lanes])
                conv.append(out)
            gate, val = conv
            y_s[:, f0 + c * LANES:f0 + (c + 1) * LANES] = (
                gate * _sigmoid(gate) * val).astype(BF16)

    hn = h_ref[...] + _dot(y_s[...], wd_ref[...])
    hn_ref[...] = hn
    an_ref[...] = _rms_rows(hn, g_ref[...]).astype(an_ref.dtype)


def ffn_res_norm(a, w_up, conv_w, conv_b, w_down, h, g_next, a_dtype, tm, tf):
    bsz, s, d = a.shape
    dff = w_down.shape[0]
    hb = tm // HALO
    tok = pl.BlockSpec((None, tm, d), lambda b, i: (b, i, 0))
    conv_b = conv_b.reshape(1, -1)
    g_next = g_next.reshape(1, d)
    return pl.pallas_call(
        functools.partial(_ffn_kernel, tf),
        grid=(bsz, s // tm),
        in_specs=[tok,
                  pl.BlockSpec((None, HALO, d),
                               lambda b, i: (b, jnp.maximum(i * hb - 1, 0), 0)),
                  _whole(w_up), _whole(conv_w), _whole(conv_b), _whole(w_down),
                  tok, _whole(g_next)],
        out_specs=[tok, tok],
        out_shape=[jax.ShapeDtypeStruct((bsz, s, d), F32),
                   jax.ShapeDtypeStruct((bsz, s, d), a_dtype)],
        scratch_shapes=[pltpu.VMEM((2, 2, tf // LANES, HALO + tm, LANES), F32),
                        pltpu.VMEM((tm, dff), BF16)],
        compiler_params=_cparams("parallel", "parallel"),
        name="ffn_res_norm",
    )(a, a, w_up, conv_w, conv_b, w_down, h, g_next)


def _pick(n, pref):
    for c in pref:
        if n % c == 0:
            return c
    return n


def kernel(x, mem, norm_mix_g, norm_xa_g, norm_mem_g, norm_ffn_g, hg_w_in, hg_w_out, hg_norm_g, hg_lb_logits, ml_w_in, ml_b_gate, ml_w_out, ml_norm_g, xa_wq, xa_wkv, xa_wo, ffn_w_up, ffn_conv_w, ffn_conv_b, ffn_w_down, final_g):
    bsz, s, d = x.shape
    n_mem = mem.shape[1]
    depth = norm_mix_g.shape[0]
    t = bsz * s
    tm = _pick(s, (512, 256, 128, 64))
    tm_proj = _pick(s, (1024, 512, 256, 128, 64))
    lblk = _pick(s, (512, 256, 128, 64))
    tn = 256
    dff = ffn_w_down.shape[1]
    tf = 512
    nqk2 = 2 * ML_HEADS * ML_DQK
    n_main = nqk2 + 2 * d
    gl = LANES

    h = x
    a = x.reshape(t, d)
    mem2d = mem.reshape(bsz * n_mem, d)
    r3 = lambda z: z.reshape(bsz, s, -1)

    for layer in range(depth):
        j = layer // 2
        if layer % 2 == 0:
            q, k, lf, v, g = hg_inproj(a, norm_mix_g[layer], hg_w_in[j].astype(BF16),
                                       hg_lb_logits, layer, tm_proj, tn)
            o = hg_chunk(r3(q), r3(k), r3(v), r3(lf), r3(g), hg_norm_g[j], lblk)
            w_out = hg_w_out[j]
        else:
            w_in = ml_w_in[j]
            w_gate = jnp.pad(w_in[:, n_main:], ((0, 0), (0, gl - 2 * ML_HEADS))).astype(BF16)
            b_gate = jnp.pad(ml_b_gate[j], (0, gl - 2 * ML_HEADS)).reshape(1, gl)
            proj, gts = ml_inproj(a, w_in[:, :n_main].astype(BF16), w_gate, b_gate,
                                  tm_proj, tn, (nqk2 + d) // tn)
            gts = r3(gts)
            g_rows = gts[:, :, :2 * ML_HEADS].reshape(bsz, s // CHUNK, CHUNK, 2 * ML_HEADS)
            g_rows = g_rows.transpose(0, 1, 3, 2)
            o = ml_chunk(r3(proj), gts, g_rows, ml_norm_g[j], lblk)
            w_out = ml_w_out[j]

        kv = mem_kv(mem2d, norm_mem_g[layer], xa_wkv[layer].astype(BF16), 512)
        h, a = mix_xattn_res_norm(o, w_out.astype(BF16), norm_xa_g[layer],
                                  kv.reshape(bsz, n_mem, 2 * d),
                                  xa_wq[layer].astype(BF16), xa_wo[layer].astype(BF16),
                                  h, norm_ffn_g[layer], tm_proj)

        last = layer == depth - 1
        g_next = final_g if last else norm_mix_g[layer + 1]
        h, a = ffn_res_norm(a, ffn_w_up[layer].astype(BF16), ffn_conv_w[layer],
                            ffn_conv_b[layer], ffn_w_down[layer].astype(BF16), h, g_next,
                            F32 if last else BF16, tm, tf)
        a = a.reshape(t, d) if not last else a

    return a.reshape(bsz, s, d)
```

```python
import functools

import jax
import jax.numpy as jnp
from jax import lax
from jax.experimental import pallas as pl
from jax.experimental.pallas import tpu as pltpu

F32 = jnp.float32
BF16 = jnp.bfloat16

EPS = 1e-6
NEG_BIG = -1e30
F_FLOOR = 1e-20
LOG2E = 1.4426950408889634
LANES = 128
CHUNK = 64
HG_D = 128
ML_HEADS = 8
ML_DQK = 64
ML_DV = 128
XA_HEADS = 4
CONV_W = 3
HALO = 16

VMEM_LIMIT = 56 * 1024 * 1024


def _cparams(*sem):
    return pltpu.CompilerParams(dimension_semantics=sem, vmem_limit_bytes=VMEM_LIMIT)


def _dot(a, b):
    return jnp.dot(a, b, preferred_element_type=F32)


def _dot_nt(a, b):
    return lax.dot_general(a, b, (((1,), (1,)), ((), ())), preferred_element_type=F32)


def _dot_tn(a, b):
    return lax.dot_general(a, b, (((0,), (0,)), ((), ())), preferred_element_type=F32)


def _split3(x):
    hi = x.astype(BF16)
    rest = x - hi.astype(F32)
    mid = rest.astype(BF16)
    lo = (rest - mid.astype(F32)).astype(BF16)
    return hi, mid, lo


def _tri_dot(tri, x):
    hi, mid, lo = _split3(x)
    return (_dot(tri, lo) + _dot(tri, mid)) + _dot(tri, hi)


def _dot_tri(x, tri):
    hi, mid, lo = _split3(x)
    return (_dot(lo, tri) + _dot(mid, tri)) + _dot(hi, tri)


def _sigmoid(x):
    return 0.5 * jnp.tanh(0.5 * x) + 0.5


def _silu(x):
    return x * _sigmoid(x)


def _sigmoid_pair(x):
    e = jnp.exp(-jnp.abs(x))
    big = 1.0 / (1.0 + e)
    small = e * big
    pos = x >= 0.0
    return jnp.where(pos, big, small), jnp.where(pos, small, big)


def _whole(arr):
    nd = arr.ndim
    return pl.BlockSpec(arr.shape, lambda *_: (0,) * nd, pipeline_mode=pl.Buffered(1))


def _pipeline3(n, prep, front, tail):
    prep(0, 0)
    if n == 1:
        front(0, 0)
        tail(0, 0)
        return
    assert n % 2 == 0
    prep(1, 1)
    front(0, 0)

    def two_steps(j, carry):
        for par in (1, 0):
            i = 2 * j + 2 - par
            tail(i - 1, 1 - par)
            prep(i + 1, 1 - par)
            front(i, par)
        return carry

    lax.fori_loop(0, (n - 2) // 2, two_steps, 0)
    tail(n - 2, 0)
    front(n - 1, 1)
    tail(n - 1, 1)


def _rms_rows(x, g):
    ms = jnp.mean(x * x, axis=-1, keepdims=True)
    return x * lax.rsqrt(ms + EPS) * g


def _hg_inproj_kernel(layer, tn, a_ref, gin_ref, w_ref, lbl_ref,
                      q_ref, k_ref, lf_ref, v_ref, g_ref):
    a = a_ref[...]
    if a.dtype != BF16:
        a = _rms_rows(a, gin_ref[...]).astype(BF16)
    d = a.shape[1]
    depth = lbl_ref.shape[0]
    for j in range(d // tn):
        cols = slice(j * tn, (j + 1) * tn)
        proj = lambda grp: _dot(a, w_ref[:, grp * d + j * tn:grp * d + (j + 1) * tn])
        q_ref[:, cols] = _silu(proj(0)).astype(BF16)
        v_ref[:, cols] = proj(2).astype(BF16)
        g_ref[:, cols] = _silu(proj(3)).astype(BF16)
        rows = [lbl_ref[l:l + 1, cols] for l in range(depth)]
        mx = functools.reduce(jnp.maximum, rows)
        es = [jnp.exp(r - mx) for r in rows]
        tot = functools.reduce(lambda x, y: x + y, es)
        ps = [e / tot for e in es]
        lb = functools.reduce(lambda x, y: x + y, ps[:layer + 1]) - ps[0]
        sig_pos, sig_neg = _sigmoid_pair(proj(1))
        f = lb + (1.0 - lb) * sig_pos
        lf_ref[:, cols] = jnp.log(jnp.maximum(f, F_FLOOR))
        k_ref[:, cols] = ((1.0 - lb) * sig_neg).astype(BF16)


def hg_inproj(a, g_in, w_in, lb_logits, layer, tm, tn):
    t, d = a.shape
    tok = pl.BlockSpec((tm, d), lambda i: (i, 0))
    g_in = g_in.reshape(1, d)
    return pl.pallas_call(
        functools.partial(_hg_inproj_kernel, layer, tn),
        grid=(t // tm,),
        in_specs=[tok, _whole(g_in), _whole(w_in), _whole(lb_logits)],
        out_specs=[tok] * 5,
        out_shape=[jax.ShapeDtypeStruct((t, d), BF16),
                   jax.ShapeDtypeStruct((t, d), BF16),
                   jax.ShapeDtypeStruct((t, d), F32),
                   jax.ShapeDtypeStruct((t, d), BF16),
                   jax.ShapeDtypeStruct((t, d), BF16)],
        compiler_params=_cparams("parallel"),
        name="hg_inproj",
    )(a, g_in, w_in, lb_logits)


def _hg_chunk_kernel(q_ref, k_ref, v_ref, lf_ref, g_ref, ng_ref, o_ref,
                     st_ref, b_buf, q_buf, k_buf, o_buf, amat_buf):
    lblk, d = q_ref.shape
    n_chunks = lblk // CHUNK
    n_grp = CHUNK // 8
    heads = [slice(h * HG_D, (h + 1) * HG_D) for h in range(d // HG_D)]

    @pl.when(pl.program_id(1) == 0)
    def _():
        st_ref[...] = jnp.zeros_like(st_ref)

    t_idx = lax.broadcasted_iota(jnp.int32, (CHUNK, CHUNK), 0)
    s_idx = lax.broadcasted_iota(jnp.int32, (CHUNK, CHUNK), 1)
    tril = (t_idx >= s_idx).astype(BF16)
    pair_code = jnp.where(t_idx >= s_idx, t_idx ^ s_idx, -1)
    grp_row = lax.broadcasted_iota(jnp.int32, (8, HG_D), 0)
    zeros_grp = jnp.zeros((8, HG_D), F32)
    splits = [1 << p for p in range((CHUNK - 1).bit_length())]

    def rows(c):
        return pl.ds(pl.multiple_of(c * CHUNK, CHUNK), CHUNK)

    def split_ref(b_s, g, m, cs):
        if 2 * m >= 8:
            r = (8 * g) // (2 * m) * (2 * m) + m - 1
            return b_s[pl.ds(r, 1), cs]
        tile = b_s[pl.ds(8 * g + m - 1, 1), cs]
        for blk in range(1, 8 // (2 * m)):
            tile = jnp.where(grp_row >= blk * 2 * m,
                             b_s[pl.ds(8 * g + blk * 2 * m + m - 1, 1), cs], tile)
        return tile

    def prep(c, par):
        rs = rows(c)
        b_buf[par] = _tri_dot(tril, lf_ref[rs, :]) * LOG2E
        q_buf[par] = q_ref[rs, :].astype(F32)
        k_buf[par] = k_ref[rs, :].astype(F32)

    def front(c, par):
        rs = rows(c)
        b_s, q_s, k_s, o_s = b_buf.at[par], q_buf.at[par], k_buf.at[par], o_buf.at[par]
        vb = v_ref[rs, :]
        b = b_s[...]
        b_last = b_s[CHUNK - 1:CHUNK, :]
        qb = (q_s[...] * jnp.exp2(b)).astype(BF16)
        k_dec = (k_s[...] * jnp.exp2(b_last - b)).astype(BF16)
        dec = jnp.exp2(b_last)
        for h, cs in enumerate(heads):
            st = st_ref[h]
            o_s[:, cs] = _dot_nt(qb[:, cs], st.astype(BF16))
            st_ref[h] = st * dec[:, cs] + _dot_tn(vb[:, cs], k_dec[:, cs])

        for h, cs in enumerate(heads):
            acc = jnp.where(pair_code == 0, _dot_nt(q_ref[rs, cs], k_ref[rs, cs]), 0.0)
            for m in splits:
                q_parts, k_parts = [], []
                for g in range(n_grp):
                    ts = slice(8 * g, 8 * g + 8)
                    whole = 2 * m >= 16
                    upper = (8 * g) % (2 * m) >= m
                    b_g, b_r = b_s[ts, cs], split_ref(b_s, g, m, cs)
                    q_parts.append(q_s[ts, cs] * jnp.exp2(b_g - b_r)
                                   if not whole or upper else zeros_grp)
                    k_parts.append(k_s[ts, cs] * jnp.exp2(b_r - b_g)
                                   if not whole or not upper else zeros_grp)
                res = _dot_nt(jnp.concatenate(q_parts, axis=0).astype(BF16),
                              jnp.concatenate(k_parts, axis=0).astype(BF16))
                acc = jnp.where(pair_code >= m, res, acc)
            amat_buf[par, h] = acc.astype(BF16)

    def tail(c, par):
        rs = rows(c)
        o_s = o_buf.at[par]
        for h, cs in enumerate(heads):
            o = o_s[:, cs] + _dot(amat_buf[par, h], v_ref[rs, cs])
            ms = jnp.mean(o * o, axis=-1, keepdims=True)
            o = o * lax.rsqrt(ms + EPS) * ng_ref[:, cs] * g_ref[rs, cs].astype(F32)
            o_ref[rs, cs] = o.astype(o_ref.dtype)

    _pipeline3(n_chunks, prep, front, tail)


def hg_chunk(q, k, v, lf, g, norm_g, lblk):
    bsz, s, d = q.shape
    n_heads = d // HG_D
    spec = pl.BlockSpec((None, lblk, d), lambda b, c: (b, c, 0))
    return pl.pallas_call(
        _hg_chunk_kernel,
        grid=(bsz, s // lblk),
        in_specs=[spec, spec, spec, spec, spec, pl.BlockSpec((1, d), lambda b, c: (0, 0))],
        out_specs=spec,
        out_shape=jax.ShapeDtypeStruct((bsz, s, d), BF16),
        scratch_shapes=[pltpu.VMEM((n_heads, HG_D, HG_D), F32),
                        pltpu.VMEM((2, CHUNK, d), F32),
                        pltpu.VMEM((2, CHUNK, d), F32),
                        pltpu.VMEM((2, CHUNK, d), F32),
                        pltpu.VMEM((2, CHUNK, d), F32),
                        pltpu.VMEM((2, n_heads, CHUNK, CHUNK), BF16)],
        compiler_params=_cparams("parallel", "arbitrary"),
        name="hg_chunk",
    )(q, k, v, lf, g, norm_g.reshape(1, d))


def _ml_inproj_kernel(n_lin, tn, a_ref, w_ref, wg_ref, bg_ref, p_ref, gt_ref):
    a = a_ref[...]
    for j in range(w_ref.shape[1] // tn):
        cols = slice(j * tn, (j + 1) * tn)
        acc = _dot(a, w_ref[:, cols])
        p_ref[:, cols] = (acc if j < n_lin else _sigmoid(acc)).astype(BF16)
    gts = _dot(a, wg_ref[...]) + bg_ref[...]
    lane = lax.broadcasted_iota(jnp.int32, gts.shape, 1)
    log_sig = jnp.minimum(gts, 0.0) - jnp.log(1.0 + jnp.exp(-jnp.abs(gts)))
    gt_ref[...] = jnp.where(lane < ML_HEADS, gts, log_sig)


def ml_inproj(a, w_main, w_gate, b_gate, tm, tn, n_lin):
    t, d = a.shape
    n = w_main.shape[1]
    gl = w_gate.shape[1]
    return pl.pallas_call(
        functools.partial(_ml_inproj_kernel, n_lin, tn),
        grid=(t // tm,),
        in_specs=[pl.BlockSpec((tm, d), lambda i: (i, 0)),
                  _whole(w_main), _whole(w_gate), _whole(b_gate)],
        out_specs=[pl.BlockSpec((tm, n), lambda i: (i, 0)),
                   pl.BlockSpec((tm, gl), lambda i: (i, 0))],
        out_shape=[jax.ShapeDtypeStruct((t, n), BF16),
                   jax.ShapeDtypeStruct((t, gl), F32)],
        compiler_params=_cparams("parallel"),
        name="ml_inproj",
    )(a, w_main, w_gate, b_gate)


def _ml_chunk_kernel(qk_ref, v_ref, og_ref, gc_ref, gr_ref, ng_ref, o_ref,
                     c_ref, m_ref):
    lblk = v_ref.shape[0]
    nqk = ML_HEADS * ML_DQK

    @pl.when(pl.program_id(1) == 0)
    def _():
        c_ref[...] = jnp.zeros_like(c_ref)
        m_ref[...] = jnp.zeros_like(m_ref)

    row = lax.broadcasted_iota(jnp.int32, (CHUNK, CHUNK), 0)
    col = lax.broadcasted_iota(jnp.int32, (CHUNK, CHUNK), 1)
    causal = row >= col
    tril = causal.astype(BF16)
    triu = (row <= col).astype(BF16)
    ones_v = jnp.ones((CHUNK, ML_DV), BF16)
    scale = ML_DQK ** -0.5

    def chunk_body(c, carry):
        r0 = pl.multiple_of(c * CHUNK, CHUNK)
        rs = pl.ds(r0, CHUNK)
        g_cols = gc_ref[rs, :]
        g_rows = gr_ref[c]
        b_cols = _tri_dot(tril, g_cols)
        b_rows = _dot_tri(g_rows, triu)
        heads = range(ML_HEADS)
        qs = [qk_ref[rs, h * ML_DQK:(h + 1) * ML_DQK] for h in heads]
        ks = [qk_ref[rs, nqk + h * ML_DQK:nqk + (h + 1) * ML_DQK] for h in heads]
        v_exts = [jnp.concatenate([v_ref[rs, h * ML_DV:(h + 1) * ML_DV], ones_v], axis=1)
                  for h in heads]
        c_prevs = [c_ref[h] for h in heads]
        b_reps = [jnp.broadcast_to(b_cols[:, ML_HEADS + h:ML_HEADS + h + 1], (CHUNK, ML_DV))
                  for h in heads]
        li_reps = [jnp.broadcast_to(g_cols[:, h:h + 1], (CHUNK, ML_DV)) for h in heads]
        s_qk = [_dot_nt(qs[h], ks[h]) for h in heads]
        q_c = [_dot(qs[h], c_prevs[h].astype(BF16)) for h in heads]
        m_ts, d_ms, w_inters, kws, decs = [], [], [], [], []
        for h in heads:
            b_rep = b_reps[h]
            b_row = b_rows[ML_HEADS + h:ML_HEADS + h + 1, :]
            li_row = g_rows[h:h + 1, :]
            m_prev = m_ref[h:h + 1, :]
            log_d = jnp.where(causal, b_rep[:, :CHUNK] - b_row + li_row, NEG_BIG)
            inter = b_rep + m_prev
            m_t = jnp.maximum(inter, jnp.max(log_d, axis=-1, keepdims=True))
            m_ts.append(m_t)
            d_ms.append(jnp.exp(log_d - m_t[:, :CHUNK]))
            w_inters.append(jnp.exp(inter - m_t))
            b_last = b_rep[CHUNK - 1:CHUNK, :]
            log_w = b_last - b_rep + li_reps[h]
            m_new = jnp.maximum(b_last + m_prev, jnp.max(log_w, axis=0, keepdims=True))
            w = jnp.exp(log_w - m_new)
            decs.append(jnp.exp(b_last + m_prev - m_new))
            kws.append((ks[h].astype(F32) * (w[:, :ML_DQK] * scale)).astype(BF16))
            m_ref[h:h + 1, :] = m_new
        nums = []
        for h in heads:
            s_mat = (s_qk[h] * scale * d_ms[h]).astype(BF16)
            w2 = jnp.concatenate([w_inters[h], w_inters[h]], axis=1)
            nums.append(_dot(s_mat, v_exts[h]) + w2 * q_c[h])
            dec2 = jnp.concatenate([decs[h], decs[h]], axis=1)
            c_ref[h] = dec2 * c_prevs[h] + _dot_tn(kws[h], v_exts[h])
        for h in heads:
            den = nums[h][:, ML_DV:]
            hh = nums[h][:, :ML_DV] / jnp.maximum(jnp.abs(den), jnp.exp(-m_ts[h]))
            vs = pl.ds(h * ML_DV, ML_DV)
            ms = jnp.mean(hh * hh, axis=-1, keepdims=True)
            out = hh * lax.rsqrt(ms + EPS) * ng_ref[:, vs] * og_ref[rs, vs].astype(F32)
            o_ref[rs, vs] = out.astype(o_ref.dtype)
        return carry

    lax.fori_loop(0, lblk // CHUNK, chunk_body, 0)


def ml_chunk(proj, gates_cols, gates_rows, norm_g, lblk):
    bsz, s, _ = proj.shape
    d = ML_HEADS * ML_DV
    nqk2 = 2 * ML_HEADS * ML_DQK
    gl = gates_cols.shape[-1]
    return pl.pallas_call(
        _ml_chunk_kernel,
        grid=(bsz, s // lblk),
        in_specs=[pl.BlockSpec((None, lblk, nqk2), lambda b, c: (b, c, 0)),
                  pl.BlockSpec((None, lblk, d), lambda b, c: (b, c, nqk2 // d)),
                  pl.BlockSpec((None, lblk, d), lambda b, c: (b, c, nqk2 // d + 1)),
                  pl.BlockSpec((None, lblk, gl), lambda b, c: (b, c, 0)),
                  pl.BlockSpec((None, lblk // CHUNK, 2 * ML_HEADS, CHUNK),
                               lambda b, c: (b, c, 0, 0)),
                  pl.BlockSpec((1, d), lambda b, c: (0, 0))],
        out_specs=pl.BlockSpec((None, lblk, d), lambda b, c: (b, c, 0)),
        out_shape=jax.ShapeDtypeStruct((bsz, s, d), BF16),
        scratch_shapes=[pltpu.VMEM((ML_HEADS, ML_DQK, 2 * ML_DV), F32),
                        pltpu.VMEM((ML_HEADS, 128), F32)],
        compiler_params=_cparams("parallel", "arbitrary"),
        name="ml_chunk",
    )(proj, proj, proj, gates_cols, gates_rows, norm_g.reshape(1, d))


def _mem_kv_kernel(m_ref, g_ref, w_ref, kv_ref):
    mn = _rms_rows(m_ref[...], g_ref[...]).astype(BF16)
    kv_ref[...] = _dot(mn, w_ref[...]).astype(BF16)


def mem_kv(mem2d, g, wkv, tn):
    t, d = mem2d.shape
    depth, _, n = wkv.shape
    return pl.pallas_call(
        _mem_kv_kernel,
        grid=(depth, n // tn),
        in_specs=[pl.BlockSpec((t, d), lambda l, j: (0, 0)),
                  pl.BlockSpec((None, 1, d), lambda l, j: (l, 0, 0)),
                  pl.BlockSpec((None, d, tn), lambda l, j: (l, 0, j))],
        out_specs=pl.BlockSpec((None, t, tn), lambda l, j: (l, 0, j)),
        out_shape=jax.ShapeDtypeStruct((depth, t, n), BF16),
        compiler_params=_cparams("parallel", "parallel"),
        name="mem_kv",
    )(mem2d, g.reshape(depth, 1, d), wkv)


def _xattn_kernel(om_ref, wm_ref, gx_ref, k_ref, v_ref, wq_ref, wo_ref, h_ref, g_ref,
                  hn_ref, an_ref):
    d = om_ref.shape[1]
    hd = d // XA_HEADS
    scale = hd ** -0.5
    h_mix = h_ref[...] + _dot(om_ref[...], wm_ref[...])
    a = _rms_rows(h_mix, gx_ref[...]).astype(BF16)
    q = _dot(a, wq_ref[...]).astype(BF16)
    outs = []
    for hh in range(XA_HEADS):
        cs = slice(hh * hd, (hh + 1) * hd)
        s = _dot_nt(q[:, cs], k_ref[:, cs]) * scale
        p = jnp.exp(s - jnp.max(s, axis=-1, keepdims=True))
        p = p / jnp.sum(p, axis=-1, keepdims=True)
        outs.append(_dot(p.astype(BF16), v_ref[:, cs]).astype(BF16))
    o = jnp.concatenate(outs, axis=1)
    hn = h_mix + _dot(o, wo_ref[...])
    hn_ref[...] = hn
    an_ref[...] = _rms_rows(hn, g_ref[...]).astype(an_ref.dtype)


def mix_xattn_res_norm(o_mix, w_mix, g_xa, kv, wq, wo, h, g_next, tm):
    bsz, s, d = o_mix.shape
    n_mem = kv.shape[1]
    tok = pl.BlockSpec((None, tm, d), lambda b, i: (b, i, 0))
    g_xa, g_next = g_xa.reshape(1, d), g_next.reshape(1, d)
    return pl.pallas_call(
        _xattn_kernel,
        grid=(bsz, s // tm),
        in_specs=[tok, _whole(w_mix), _whole(g_xa),
                  pl.BlockSpec((None, n_mem, d), lambda b, i: (b, 0, 0)),
                  pl.BlockSpec((None, n_mem, d), lambda b, i: (b, 0, 1)),
                  _whole(wq), _whole(wo), tok, _whole(g_next)],
        out_specs=[tok, tok],
        out_shape=[jax.ShapeDtypeStruct((bsz, s, d), F32),
                   jax.ShapeDtypeStruct((bsz, s, d), BF16)],
        compiler_params=_cparams("parallel", "parallel"),
        name="mix_xattn_res_norm",
    )(o_mix, w_mix, g_xa, kv, kv, wq, wo, h, g_next)


def _ffn_kernel(tf, a_ref, ah_ref, wup_ref, cw_ref, cb_ref, wd_ref, h_ref, g_ref,
                hn_ref, an_ref, u_s, y_s):
    i = pl.program_id(1)
    tm = a_ref.shape[0]
    dff = wd_ref.shape[0]
    tiles = [(f0, min(tf, dff - f0)) for f0 in range(0, dff, tf)]

    halo = jnp.where(i > 0, ah_ref[...], jnp.zeros_like(ah_ref))
    a_ext = jnp.concatenate([halo, a_ref[...]], axis=0)

    for j, (f0, fw) in enumerate(tiles):
        buf = j % 2
        n_lane = fw // LANES
        for part in range(2):
            col = part * dff + f0
            u = _dot(a_ext, wup_ref[:, col:col + fw])
            for c in range(n_lane):
                u_s[buf, part, c] = u[:, c * LANES:(c + 1) * LANES]
        for c in range(n_lane):
            conv = []
            for part in range(2):
                lanes = slice(part * dff + f0 + c * LANES,
                              part * dff + f0 + (c + 1) * LANES)
                out = cb_ref[:, lanes]
                for tap in range(CONV_W):
                    sh = CONV_W - 1 - tap
                    out = out + (u_s[buf, part, c, HALO - sh:HALO - sh + tm, :]
                                 * cw_ref[tap:tap + 1, lanes])
                conv.append(out)
            gate, val = conv
            y_s[:, f0 + c * LANES:f0 + (c + 1) * LANES] = (
                gate * _sigmoid(gate) * val).astype(BF16)

    hn = h_ref[...] + _dot(y_s[...], wd_ref[...])
    hn_ref[...] = hn
    an_ref[...] = _rms_rows(hn, g_ref[...]).astype(an_ref.dtype)


def ffn_res_norm(a, w_up, conv_w, conv_b, w_down, h, g_next, a_dtype, tm, tf):
    bsz, s, d = a.shape
    dff = w_down.shape[0]
    hb = tm // HALO
    tok = pl.BlockSpec((None, tm, d), lambda b, i: (b, i, 0))
    conv_b = conv_b.reshape(1, -1)
    g_next = g_next.reshape(1, d)
    return pl.pallas_call(
        functools.partial(_ffn_kernel, tf),
        grid=(bsz, s // tm),
        in_specs=[tok,
                  pl.BlockSpec((None, HALO, d),
                               lambda b, i: (b, jnp.maximum(i * hb - 1, 0), 0)),
                  _whole(w_up), _whole(conv_w), _whole(conv_b), _whole(w_down),
                  tok, _whole(g_next)],
        out_specs=[tok, tok],
        out_shape=[jax.ShapeDtypeStruct((bsz, s, d), F32),
                   jax.ShapeDtypeStruct((bsz, s, d), a_dtype)],
        scratch_shapes=[pltpu.VMEM((2, 2, tf // LANES, HALO + tm, LANES), F32),
                        pltpu.VMEM((tm, dff), BF16)],
        compiler_params=_cparams("parallel", "parallel"),
        name="ffn_res_norm",
    )(a, a, w_up, conv_w, conv_b, w_down, h, g_next)


def _pick(n, pref):
    for c in pref:
        if n % c == 0:
            return c
    return n


def kernel(x, mem, norm_mix_g, norm_xa_g, norm_mem_g, norm_ffn_g, hg_w_in, hg_w_out, hg_norm_g, hg_lb_logits, ml_w_in, ml_b_gate, ml_w_out, ml_norm_g, xa_wq, xa_wkv, xa_wo, ffn_w_up, ffn_conv_w, ffn_conv_b, ffn_w_down, final_g):
    bsz, s, d = x.shape
    n_mem = mem.shape[1]
    depth = norm_mix_g.shape[0]
    t = bsz * s
    tm = _pick(s, (512, 256, 128, 64))
    tm_proj = _pick(s, (1024, 512, 256, 128, 64))
    lblk = _pick(s, (512, 256, 128, 64))
    tn = 256
    dff = ffn_w_down.shape[1]
    tf = 512
    nqk2 = 2 * ML_HEADS * ML_DQK
    n_main = nqk2 + 2 * d
    gl = LANES

    h = x
    a = x.reshape(t, d)
    mem2d = mem.reshape(bsz * n_mem, d)
    r3 = lambda z: z.reshape(bsz, s, -1)
    kv_all = mem_kv(mem2d, norm_mem_g, xa_wkv.astype(BF16), 512)

    for layer in range(depth):
        j = layer // 2
        if layer % 2 == 0:
            q, k, lf, v, g = hg_inproj(a, norm_mix_g[layer], hg_w_in[j].astype(BF16),
                                       hg_lb_logits, layer, tm_proj, tn)
            o = hg_chunk(r3(q), r3(k), r3(v), r3(lf), r3(g), hg_norm_g[j], lblk)
            w_out = hg_w_out[j]
        else:
            w_in = ml_w_in[j]
            w_gate = jnp.pad(w_in[:, n_main:], ((0, 0), (0, gl - 2 * ML_HEADS))).astype(BF16)
            b_gate = jnp.pad(ml_b_gate[j], (0, gl - 2 * ML_HEADS)).reshape(1, gl)
            proj, gts = ml_inproj(a, w_in[:, :n_main].astype(BF16), w_gate, b_gate,
                                  tm_proj, tn, (nqk2 + d) // tn)
            gts = r3(gts)
            g_rows = gts[:, :, :2 * ML_HEADS].reshape(bsz, s // CHUNK, CHUNK, 2 * ML_HEADS)
            g_rows = g_rows.transpose(0, 1, 3, 2)
            o = ml_chunk(r3(proj), gts, g_rows, ml_norm_g[j], lblk)
            w_out = ml_w_out[j]

        kv = kv_all[layer]
        h, a = mix_xattn_res_norm(o, w_out.astype(BF16), norm_xa_g[layer],
                                  kv.reshape(bsz, n_mem, 2 * d),
                                  xa_wq[layer].astype(BF16), xa_wo[layer].astype(BF16),
                                  h, norm_ffn_g[layer], tm_proj)

        last = layer == depth - 1
        g_next = final_g if last else norm_mix_g[layer + 1]
        h, a = ffn_res_norm(a, ffn_w_up[layer].astype(BF16), ffn_conv_w[layer],
                            ffn_conv_b[layer], ffn_w_down[layer].astype(BF16), h, g_next,
                            F32 if last else BF16, tm, tf)
        a = a.reshape(t, d) if not last else a

    return a.reshape(bsz, s, d)
```
